```python
import jax, jax.numpy as jnp
from jax import lax
import numpy as np

D_MODEL = 2048
BATCH = 4
SEQ = 8192
DEPTH = 2

N_MIXERS = 2
N_LAYERS_A = (DEPTH + 1) // 2
N_LAYERS_B = DEPTH // 2
MLSTM_HEADS = 8
QK_HEAD_DIM = D_MODEL // MLSTM_HEADS // 2
V_HEAD_DIM = D_MODEL // MLSTM_HEADS
QK_WIDTH = MLSTM_HEADS * QK_HEAD_DIM
V_WIDTH = MLSTM_HEADS * V_HEAD_DIM
A_IN_WIDTH = 2 * QK_WIDTH + 2 * V_WIDTH + 2 * MLSTM_HEADS
MLSTM_CHUNK = 64
GATE_SOFTCAP = 15.0
CONV_WIDTH = 3
N_EXPERTS = 32
TOP_K = 4
D_EXPERT = D_MODEL
SWIGLU_ALPHA = 1.702
SWIGLU_LIMIT = 7.0
MOE_BLOCK = 512
PLE_DIM = 256
LN_EPS = 1e-5
RMS_EPS = 1e-6
DEEPNORM_ALPHA = (2 * DEPTH) ** 0.25
DEEPNORM_BETA = (8 * DEPTH) ** -0.25

kernel_name = "hybrid_mlstm_shortconv_moe_deepnorm"


def layer_norm(x, g, b):
    xf = x.astype(jnp.float32)
    mu = jnp.mean(xf, axis=-1, keepdims=True)
    xc = xf - mu
    var = jnp.mean(xc * xc, axis=-1, keepdims=True)
    y = xc * lax.rsqrt(var + LN_EPS) * g.astype(jnp.float32) + b.astype(jnp.float32)
    return y.astype(x.dtype)


def mlstm_cell(q, k, v, li, lf):
    bsz, nh, s, dk = q.shape
    dv = v.shape[-1]
    L = MLSTM_CHUNK
    nc = s // L

    def chunks(t):
        return jnp.moveaxis(t.reshape(bsz, nh, nc, L, *t.shape[3:]), 2, 0)

    causal = jnp.tril(jnp.ones((L, L), dtype=bool))

    def step(carry, inp):
        C, n, m = carry
        qc, kc, vc, lic, lfc = inp
        b = jnp.cumsum(lfc, axis=-1)
        dlog = b[..., :, None] - b[..., None, :] + lic[..., None, :]
        dlog = jnp.where(causal, dlog, -jnp.inf)
        inter_log = b + m[..., None]
        m_t = jnp.maximum(jnp.max(dlog, axis=-1), inter_log)
        s_w = jnp.einsum('bhtd,bhsd->bhts', qc, kc) * jnp.exp(dlog - m_t[..., None])
        inter = jnp.exp(inter_log - m_t)
        num = jnp.einsum('bhts,bhsv->bhtv', s_w, vc) + inter[..., None] * jnp.einsum('bhtd,bhdv->bhtv', qc, C)
        den = jnp.sum(s_w, axis=-1) + inter * jnp.einsum('bhtd,bhd->bht', qc, n)
        h = num / jnp.maximum(jnp.abs(den), jnp.exp(-m_t))[..., None]
        b_last = b[..., -1]
        wlog = b_last[..., None] - b + lic
        m_new = jnp.maximum(b_last + m, jnp.max(wlog, axis=-1))
        wk = jnp.exp(wlog - m_new[..., None])
        decay = jnp.exp(b_last + m - m_new)
        C_new = decay[..., None, None] * C + jnp.einsum('bhs,bhsd,bhsv->bhdv', wk, kc, vc)
        n_new = decay[..., None] * n + jnp.einsum('bhs,bhsd->bhd', wk, kc)
        return (C_new, n_new, m_new), h

    init = (jnp.zeros((bsz, nh, dk, dv), jnp.float32),
            jnp.zeros((bsz, nh, dk), jnp.float32),
            jnp.zeros((bsz, nh), jnp.float32))
    _, h = lax.scan(step, init, (chunks(q), chunks(k), chunks(v), chunks(li), chunks(lf)))
    return jnp.moveaxis(h, 0, 2).reshape(bsz, nh, s, dv)


def mlstm_mixer(x, w_in, b_gate, norm_g, w_out):
    bsz, s, _ = x.shape
    proj = x @ w_in
    q, k, v, o, g = jnp.split(proj, [QK_WIDTH, 2 * QK_WIDTH, 2 * QK_WIDTH + V_WIDTH,
                                     2 * QK_WIDTH + 2 * V_WIDTH], axis=-1)
    g = (g + b_gate).astype(jnp.float32)
    g = GATE_SOFTCAP * jnp.tanh(g / GATE_SOFTCAP)
    li = jnp.transpose(g[..., :MLSTM_HEADS], (0, 2, 1))
    lf = jnp.transpose(jax.nn.log_sigmoid(g[..., MLSTM_HEADS:]), (0, 2, 1))

    def heads(t, d):
        return jnp.transpose(t.reshape(bsz, s, MLSTM_HEADS, d), (0, 2, 1, 3)).astype(jnp.float32)

    h = mlstm_cell(heads(q, QK_HEAD_DIM) * (QK_HEAD_DIM ** -0.5), heads(k, QK_HEAD_DIM),
                   heads(v, V_HEAD_DIM), li, lf)
    h = h * lax.rsqrt(jnp.mean(h * h, axis=-1, keepdims=True) + RMS_EPS)
    h = jnp.transpose(h, (0, 2, 1, 3)).reshape(bsz, s, V_WIDTH) * norm_g.astype(jnp.float32)
    h = (h * jax.nn.sigmoid(o.astype(jnp.float32))).astype(x.dtype)
    return h @ w_out


def short_conv_mixer(x, w_in, conv_w, w_out):
    b_gate, c_gate, u = jnp.split(x @ w_in, 3, axis=-1)
    z = c_gate * u
    z = lax.conv_general_dilated(z, conv_w[:, None, :].astype(z.dtype), window_strides=(1,),
                                 padding=[(CONV_WIDTH - 1, 0)],
                                 dimension_numbers=('NWC', 'WIO', 'NWC'),
                                 feature_group_count=z.shape[-1])
    return (b_gate * z) @ w_out


def moe(x, w_router, b_router, w_gu, b_gu, w_dn, b_dn):
    bsz, s, d = x.shape
    n_tok = bsz * s
    xf = x.reshape(n_tok, d)
    logits = xf.astype(jnp.float32) @ w_router.astype(jnp.float32) + b_router.astype(jnp.float32)
    top_v, top_e = lax.top_k(logits, TOP_K)
    gate = jax.nn.softmax(top_v, axis=-1).astype(x.dtype)
    nk = n_tok * TOP_K
    e_flat = top_e.reshape(nk).astype(jnp.int32)
    tok_flat = jnp.arange(nk, dtype=jnp.int32) // TOP_K
    g_flat = gate.reshape(nk)
    order = jnp.argsort(e_flat, stable=True)
    e_sorted = e_flat[order]
    counts = jnp.bincount(e_flat, length=N_EXPERTS).astype(jnp.int32)
    padded = (counts + MOE_BLOCK - 1) // MOE_BLOCK * MOE_BLOCK
    pad_end = jnp.cumsum(padded)
    pad_start = pad_end - padded
    grp_start = jnp.cumsum(counts) - counts
    dest = pad_start[e_sorted] + jnp.arange(nk, dtype=jnp.int32) - grp_start[e_sorted]
    n_rows = -(-nk // MOE_BLOCK) * MOE_BLOCK + N_EXPERTS * MOE_BLOCK
    n_blocks = n_rows // MOE_BLOCK
    row_tok = jnp.zeros((n_rows,), jnp.int32).at[dest].set(tok_flat[order])
    row_gate = jnp.zeros((n_rows,), x.dtype).at[dest].set(g_flat[order])
    blk_start = jnp.arange(n_blocks, dtype=jnp.int32) * MOE_BLOCK
    blk_e = jnp.clip(jnp.searchsorted(pad_end, blk_start, side='right'), 0, N_EXPERTS - 1)

    def expert_block(y, blk):
        tok, gw, e = blk
        h = xf[tok] @ w_gu[e] + b_gu[e]
        g_ = jnp.minimum(h[:, :D_EXPERT], SWIGLU_LIMIT)
        up = jnp.clip(h[:, D_EXPERT:], -SWIGLU_LIMIT, SWIGLU_LIMIT)
        a = (up + 1) * (g_ * jax.nn.sigmoid(SWIGLU_ALPHA * g_))
        out = a @ w_dn[e] + b_dn[e]
        return y.at[tok].add(out * gw[:, None]), None

    y, _ = lax.scan(expert_block, jnp.zeros_like(xf),
                    (row_tok.reshape(n_blocks, MOE_BLOCK), row_gate.reshape(n_blocks, MOE_BLOCK), blk_e))
    return y.reshape(bsz, s, d)


def setup_inputs(seed: int = 0) -> dict:
    key = jax.random.key(seed)
    ks = jax.random.split(key, 24)
    f32 = jnp.float32

    def nrm(k, shape, scale):
        return jax.random.normal(k, shape, f32) * scale

    D, E, F, H = D_MODEL, N_EXPERTS, D_EXPERT, MLSTM_HEADS
    b_gate_a = jnp.concatenate([nrm(ks[5], (N_LAYERS_A, H), 0.1),
                                3.0 + nrm(ks[6], (N_LAYERS_A, H), 0.5)], axis=-1)
    return {
        "x": nrm(ks[0], (BATCH, SEQ, D), 1.0),
        "p": nrm(ks[1], (DEPTH, BATCH, SEQ, PLE_DIM), 1.0),
        "ln_g": 1.0 + nrm(ks[2], (DEPTH, 2, D), 0.02),
        "ln_b": nrm(ks[3], (DEPTH, 2, D), 0.02),
        "w_in_a": nrm(ks[4], (N_LAYERS_A, D, A_IN_WIDTH), D ** -0.5),
        "b_gate_a": b_gate_a,
        "norm_a": 1.0 + nrm(ks[7], (N_LAYERS_A, V_WIDTH), 0.02),
        "w_out_a": nrm(ks[8], (N_LAYERS_A, V_WIDTH, D), V_WIDTH ** -0.5 * DEEPNORM_BETA),
        "w_in_b": nrm(ks[9], (N_LAYERS_B, D, 3 * D), D ** -0.5),
        "conv_b": nrm(ks[10], (N_LAYERS_B, CONV_WIDTH, D), CONV_WIDTH ** -0.5),
        "w_out_b": nrm(ks[11], (N_LAYERS_B, D, D), D ** -0.5 * DEEPNORM_BETA),
        "w_router": nrm(ks[12], (DEPTH, D, E), D ** -0.5),
        "b_router": nrm(ks[13], (DEPTH, E), 0.01),
        "w_gu": nrm(ks[14], (DEPTH, E, D, 2 * F), D ** -0.5),
        "b_gu": nrm(ks[15], (DEPTH, E, 2 * F), 0.01),
        "w_dn": nrm(ks[16], (DEPTH, E, F, D), F ** -0.5 * DEEPNORM_BETA),
        "b_dn": nrm(ks[17], (DEPTH, E, D), 0.01),
        "w_ple_gate": nrm(ks[18], (DEPTH, D, D), D ** -0.5),
        "w_ple_proj": nrm(ks[19], (DEPTH, PLE_DIM, D), PLE_DIM ** -0.5),
    }


def reference(x, p, ln_g, ln_b, w_in_a, b_gate_a, norm_a, w_out_a, w_in_b, conv_b, w_out_b,
              w_router, b_router, w_gu, b_gu, w_dn, b_dn, w_ple_gate, w_ple_proj):
    for i in range(DEPTH):
        j = i // N_MIXERS
        if i % N_MIXERS == 0:
            mix = mlstm_mixer(x, w_in_a[j], b_gate_a[j], norm_a[j], w_out_a[j])
        else:
            mix = short_conv_mixer(x, w_in_b[j], conv_b[j], w_out_b[j])
        x = layer_norm(DEEPNORM_ALPHA * x + mix, ln_g[i, 0], ln_b[i, 0])
        ffn = moe(x, w_router[i], b_router[i], w_gu[i], b_gu[i], w_dn[i], b_dn[i])
        x = layer_norm(DEEPNORM_ALPHA * x + ffn, ln_g[i, 1], ln_b[i, 1])
        x = x + jax.nn.sigmoid(x @ w_ple_gate[i]) * (p[i] @ w_ple_proj[i])
    return x
```

```python
import functools

import jax
import jax.numpy as jnp
from jax import lax
from jax.experimental import pallas as pl
from jax.experimental.pallas import tpu as pltpu

F32 = jnp.float32
BF16 = jnp.bfloat16
U32 = jnp.uint32
I32 = jnp.int32

GATE_SOFTCAP = 15.0
CONV_WIDTH = 3
TOP_K = 4
SWIGLU_ALPHA = 1.702
SWIGLU_LIMIT = 7.0
LN_EPS = 1e-5
RMS_EPS = 1e-6

LANES = 128
SUBLANES = 8
VMEM_LIMIT_BYTES = 60000 * 1024

MLSTM_ROWS = 256
PROJ_TM = 512
PROJ_TN_A = 1024
PROJ_TN_B = 512
POST_TM = 256
POS_TM = 512
DISPATCH_TM = 512
EXPERT_ROWS = 512
EXPERT_TF = 512
COMBINE_TM = 256


def _dot(a, b):
    return jnp.dot(a, b, preferred_element_type=F32)


def _dot_nt(a, b):
    return lax.dot_general(a, b, (((1,), (1,)), ((), ())), preferred_element_type=F32)


def _dot_tn(a, b):
    return lax.dot_general(a, b, (((0,), (0,)), ((), ())), preferred_element_type=F32)


def _split2(x):
    hi = x.astype(BF16)
    lo = (x - hi.astype(F32)).astype(BF16)
    return hi, lo


def _split3(x):
    hi = x.astype(BF16)
    r = x - hi.astype(F32)
    mid = r.astype(BF16)
    lo = (r - mid.astype(F32)).astype(BF16)
    return hi, mid, lo


def _pack_halves(y):
    half = y.shape[1] // 2
    lo = lax.bitcast_convert_type(y[:, :half].astype(BF16).astype(F32), U32) >> 16
    hi = lax.bitcast_convert_type(y[:, half:].astype(BF16).astype(F32), U32) & jnp.uint32(0xFFFF0000)
    return lo | hi


def _unpack_halves(u):
    lo = lax.bitcast_convert_type(u << 16, F32)
    hi = lax.bitcast_convert_type(u & jnp.uint32(0xFFFF0000), F32)
    return lo, hi


def _layer_norm(y, g, b):
    mu = jnp.mean(y, axis=-1, keepdims=True)
    yc = y - mu
    var = jnp.mean(yc * yc, axis=-1, keepdims=True)
    return yc * lax.rsqrt(var + LN_EPS) * g + b


def _params(n_axes):
    return pltpu.CompilerParams(dimension_semantics=("arbitrary",) * n_axes,
                                vmem_limit_bytes=VMEM_LIMIT_BYTES)


def _gate_kernel(x_ref, w_ref, b_ref, col_ref, row_ref, *, heads):
    rows = x_ref.shape[0]
    xh, xl = _split2(x_ref[...])
    wh, wl = _split2(w_ref[...])
    g = _dot(xh, wh) + _dot(xh, wl) + _dot(xl, wh) + b_ref[...]
    g = GATE_SOFTCAP * jnp.tanh(g / GATE_SOFTCAP)
    lf = -jnp.log1p(jnp.exp(-g))
    r_i = lax.broadcasted_iota(I32, (rows, rows), 0)
    c_i = lax.broadcasted_iota(I32, (rows, rows), 1)
    tril = (c_i <= r_i).astype(BF16)
    p0, p1, p2 = _split3(lf)
    bsum = _dot(tril, p0) + _dot(tril, p1) + _dot(tril, p2)
    lane = lax.broadcasted_iota(I32, g.shape, 1)
    col = jnp.where(lane < heads, g, jnp.where(lane < 2 * heads, bsum, 0.0))
    col_ref[...] = col
    row_ref[...] = col.T[: row_ref.shape[0]]


def _mlstm_gates(xf, w_g, b_g, heads):
    n, d = xf.shape
    rows = MLSTM_ROWS
    r_pad = -(-2 * heads // SUBLANES) * SUBLANES
    wg = jnp.zeros((d, LANES), F32).at[:, : 2 * heads].set(w_g)
    bg = jnp.zeros((1, LANES), F32).at[0, : 2 * heads].set(b_g)
    return pl.pallas_call(
        functools.partial(_gate_kernel, heads=heads),
        grid=(n // rows,),
        in_specs=[pl.BlockSpec((rows, d), lambda i: (i, 0)),
                  pl.BlockSpec((d, LANES), lambda i: (0, 0)),
                  pl.BlockSpec((1, LANES), lambda i: (0, 0))],
        out_specs=[pl.BlockSpec((rows, LANES), lambda i: (i, 0)),
                   pl.BlockSpec((r_pad, rows), lambda i: (0, i))],
        out_shape=[jax.ShapeDtypeStruct((n, LANES), F32),
                   jax.ShapeDtypeStruct((r_pad, n), F32)],
        compiler_params=_params(1),
    )(xf, wg, bg)


def _proj_kernel(x_ref, w_ref, o_ref, xb_ref):
    @pl.when(pl.program_id(1) == 0)
    def _():
        xb_ref[...] = x_ref[...].astype(BF16)

    o_ref[...] = _dot(xb_ref[...], w_ref[...]).astype(o_ref.dtype)


def _proj(xf, w, tm, tn):
    n, d = xf.shape
    m = w.shape[1]
    while m % tn:
        tn //= 2
    return pl.pallas_call(
        _proj_kernel,
        grid=(n // tm, m // tn),
        in_specs=[pl.BlockSpec((tm, d), lambda i, j: (i, 0)),
                  pl.BlockSpec((d, tn), lambda i, j: (0, j))],
        out_specs=pl.BlockSpec((tm, tn), lambda i, j: (i, j)),
        out_shape=jax.ShapeDtypeStruct((n, m), BF16),
        scratch_shapes=[pltpu.VMEM((tm, d), BF16)],
        compiler_params=_params(2),
    )(xf, w)


def _mlstm_kernel(q_ref, k_ref, v_ref, o_ref, col_ref, row_ref, norm_ref, out_ref, c_ref, n_ref, *, heads, dk, dv):
    rows = q_ref.shape[0]
    scale = dk ** -0.5

    @pl.when(pl.program_id(1) == 0)
    def _():
        c_ref[...] = jnp.zeros_like(c_ref)
        n_ref[...] = jnp.zeros_like(n_ref)

    col = col_ref[...]
    row = row_ref[...]
    t_i = lax.broadcasted_iota(I32, (rows, rows), 0)
    s_i = lax.broadcasted_iota(I32, (rows, rows), 1)
    causal = s_i <= t_i
    for h in range(heads):
        q = q_ref[:, h * dk:(h + 1) * dk]
        k = k_ref[:, h * dk:(h + 1) * dk]
        v = v_ref[:, h * dv:(h + 1) * dv]
        li_c = col[:, h:h + 1]
        b_c = col[:, heads + h:heads + h + 1]
        li_r = row[h:h + 1, :]
        b_r = row[heads + h:heads + h + 1, :]
        b_last = b_c[rows - 1:rows, :]
        decay_w = jnp.where(causal, jnp.exp(b_c - b_r + li_r), 0.0)
        sw = _dot_nt(q, k) * scale * decay_w
        eb = jnp.exp(b_c) * scale
        c_prev = c_ref[h]
        n_prev = n_ref[h]
        num = _dot(sw.astype(BF16), v) + eb * _dot(q, c_prev.astype(BF16))
        qn = jnp.sum(q.astype(F32) * n_prev, axis=-1, keepdims=True)
        den = jnp.sum(sw, axis=-1, keepdims=True) + eb * qn
        hh = num / jnp.maximum(jnp.abs(den), 1.0)
        wk = jnp.exp(b_last - b_c + li_c)
        carry = jnp.exp(b_last)
        kf = k.astype(F32) * wk
        c_ref[h] = carry * c_prev + _dot_tn(kf.astype(BF16), v)
        n_ref[h] = carry * n_prev + jnp.sum(kf, axis=0, keepdims=True)
        hh = hh * lax.rsqrt(jnp.mean(hh * hh, axis=-1, keepdims=True) + RMS_EPS)
        hh = hh * norm_ref[:, h * dv:(h + 1) * dv]
        og = o_ref[:, h * dv:(h + 1) * dv].astype(F32)
        out_ref[:, h * dv:(h + 1) * dv] = (hh * jax.nn.sigmoid(og)).astype(out_ref.dtype)


def _mlstm(proj, col, row, norm, bsz, seq, heads, dk, dv):
    rows = MLSTM_ROWS
    nc = seq // rows
    qk_w, v_w = heads * dk, heads * dv
    assert v_w == 2 * qk_w
    r_pad = row.shape[0]
    tok = lambda b, c: b * nc + c
    return pl.pallas_call(
        functools.partial(_mlstm_kernel, heads=heads, dk=dk, dv=dv),
        grid=(bsz, nc),
        in_specs=[pl.BlockSpec((rows, qk_w), lambda b, c: (tok(b, c), 0)),
                  pl.BlockSpec((rows, qk_w), lambda b, c: (tok(b, c), 1)),
                  pl.BlockSpec((rows, v_w), lambda b, c: (tok(b, c), 1)),
                  pl.BlockSpec((rows, v_w), lambda b, c: (tok(b, c), 2)),
                  pl.BlockSpec((rows, LANES), lambda b, c: (tok(b, c), 0)),
                  pl.BlockSpec((r_pad, rows), lambda b, c: (0, tok(b, c))),
                  pl.BlockSpec((1, v_w), lambda b, c: (0, 0))],
        out_specs=pl.BlockSpec((rows, v_w), lambda b, c: (tok(b, c), 0)),
        out_shape=jax.ShapeDtypeStruct((bsz * seq, v_w), BF16),
        scratch_shapes=[pltpu.VMEM((heads, dk, dv), F32), pltpu.VMEM((heads, 1, dk), F32)],
        compiler_params=_params(2),
    )(proj, proj, proj, proj, col, row, norm.reshape(1, v_w))


def _conv_kernel(x_ref, wb_ref, wc_ref, wu_ref, cw_ref, o_ref, xb_ref, carry_ref, *, steps_per_seq):
    i, j = pl.program_id(0), pl.program_id(1)
    tm = x_ref.shape[0]

    @pl.when(j == 0)
    def _():
        xb_ref[...] = x_ref[...].astype(BF16)

    xb = xb_ref[...]
    z = _dot(xb, wc_ref[...]) * _dot(xb, wu_ref[...])
    @pl.when(i % steps_per_seq == 0)
    def _():
        carry_ref[j] = jnp.zeros(carry_ref.shape[1:], F32)

    prev = carry_ref[j]
    carry_ref[j] = z[tm - SUBLANES:, :]
    r_i = lax.broadcasted_iota(I32, z.shape, 0)
    z1 = jnp.where(r_i == 0, prev[SUBLANES - 1:SUBLANES, :], pltpu.roll(z, 1, 0))
    z2 = jnp.where(r_i == 0, prev[SUBLANES - 2:SUBLANES - 1, :],
                   jnp.where(r_i == 1, prev[SUBLANES - 1:SUBLANES, :], pltpu.roll(z, 2, 0)))
    cw = cw_ref[...]
    zc = cw[0:1, :] * z2 + cw[1:2, :] * z1 + cw[2:3, :] * z
    o_ref[...] = (_dot(xb, wb_ref[...]) * zc).astype(o_ref.dtype)


def _conv_front(xf, w_in, conv_w, seq):
    n, d = xf.shape
    tm, tn = PROJ_TM, PROJ_TN_B
    nj = d // tn
    cw = jnp.zeros((SUBLANES, d), F32).at[:CONV_WIDTH].set(conv_w)
    return pl.pallas_call(
        functools.partial(_conv_kernel, steps_per_seq=seq // tm),
        grid=(n // tm, nj),
        in_specs=[pl.BlockSpec((tm, d), lambda i, j: (i, 0)),
                  pl.BlockSpec((d, tn), lambda i, j: (0, j)),
                  pl.BlockSpec((d, tn), lambda i, j: (0, nj + j)),
                  pl.BlockSpec((d, tn), lambda i, j: (0, 2 * nj + j)),
                  pl.BlockSpec((SUBLANES, tn), lambda i, j: (0, j))],
        out_specs=pl.BlockSpec((tm, tn), lambda i, j: (i, j)),
        out_shape=jax.ShapeDtypeStruct((n, d), BF16),
        scratch_shapes=[pltpu.VMEM((tm, d), BF16), pltpu.VMEM((nj, SUBLANES, tn), F32)],
        compiler_params=_params(2),
    )(xf, w_in, w_in, w_in, cw)


def _post_kernel(a_ref, w_ref, x_ref, g_ref, b_ref, wr_ref, br_ref,
                 x1_ref, x1p_ref, topi_ref, gcol_ref, cnt_ref, *, alpha, e_pad):
    tm = a_ref.shape[0]
    y = alpha * x_ref[...] + _dot(a_ref[...], w_ref[...])
    x1 = _layer_norm(y, g_ref[...], b_ref[...])
    x1_ref[...] = x1
    x1p_ref[...] = _pack_halves(x1)
    xh, xl = _split2(x1)
    w2 = wr_ref[...]
    l2 = _dot(xh, w2)
    logits = l2[:, :LANES] + l2[:, LANES:] + _dot(xl, w2[:, :LANES]) + br_ref[...]
    lt = logits.T[:e_pad]
    e_i = lax.broadcasted_iota(I32, lt.shape, 0)
    cur = lt
    ids, vals = [], []
    for _ in range(TOP_K):
        m = jnp.max(cur, axis=0, keepdims=True)
        idx = jnp.min(jnp.where(cur == m, e_i, e_pad), axis=0, keepdims=True)
        cur = jnp.where(e_i == idx, -jnp.inf, cur)
        ids.append(idx)
        vals.append(m)
    ex = [jnp.exp(v - vals[0]) for v in vals]
    den = ex[0] + ex[1] + ex[2] + ex[3]
    k8 = lax.broadcasted_iota(I32, (SUBLANES, tm), 0)
    k128 = lax.broadcasted_iota(I32, (LANES, tm), 0)
    topi = jnp.zeros((SUBLANES, tm), I32)
    gmat = jnp.zeros((LANES, tm), F32)
    for kk in range(TOP_K):
        topi = jnp.where(k8 == kk, ids[kk], topi)
        gmat = jnp.where(k128 == kk, ex[kk] / den, gmat)
    topi_ref[...] = topi
    gcol_ref[...] = gmat.T
    sel = (cur == -jnp.inf).astype(F32)
    cnt_ref[0] = jnp.broadcast_to(jnp.sum(sel, axis=1, keepdims=True), (e_pad, LANES))


def _post_mixer(a, w_out, xf, ln_g, ln_b, w_router, b_router, alpha):
    n, d = xf.shape
    tm = POST_TM
    e = w_router.shape[1]
    assert TOP_K <= e <= LANES
    e_pad = -(-e // SUBLANES) * SUBLANES
    wr = jnp.zeros((d, LANES), F32).at[:, :e].set(w_router)
    wr_hi = wr.astype(BF16)
    wr_lo = (wr - wr_hi.astype(F32)).astype(BF16)
    wr2 = jnp.concatenate([wr_hi, wr_lo], axis=1)
    br = jnp.full((1, LANES), -1e30, F32).at[0, :e].set(b_router)
    full = lambda i: (0, 0)
    return pl.pallas_call(
        functools.partial(_post_kernel, alpha=alpha, e_pad=e_pad),
        grid=(n // tm,),
        in_specs=[pl.BlockSpec((tm, d), lambda i: (i, 0)),
                  pl.BlockSpec((d, d), full),
                  pl.BlockSpec((tm, d), lambda i: (i, 0)),
                  pl.BlockSpec((1, d), full),
                  pl.BlockSpec((1, d), full),
                  pl.BlockSpec((d, 2 * LANES), full),
                  pl.BlockSpec((1, LANES), full)],
        out_specs=[pl.BlockSpec((tm, d), lambda i: (i, 0)),
                   pl.BlockSpec((tm, d // 2), lambda i: (i, 0)),
                   pl.BlockSpec((SUBLANES, tm), lambda i: (0, i)),
                   pl.BlockSpec((tm, LANES), lambda i: (i, 0)),
                   pl.BlockSpec((1, e_pad, LANES), lambda i: (i, 0, 0))],
        out_shape=[jax.ShapeDtypeStruct((n, d), F32),
                   jax.ShapeDtypeStruct((n, d // 2), U32),
                   jax.ShapeDtypeStruct((SUBLANES, n), I32),
                   jax.ShapeDtypeStruct((n, LANES), F32),
                   jax.ShapeDtypeStruct((n // tm, e_pad, LANES), F32)],
        compiler_params=_params(1),
    )(a, w_out, xf, ln_g.reshape(1, d), ln_b.reshape(1, d), wr2, br)


def _pos_kernel(topi_ref, off_ref, pos_ref, *, e_pad):
    tm = topi_ref.shape[1]
    ti = topi_ref[...]
    e_i = lax.broadcasted_iota(I32, (e_pad, tm), 0)
    hits = [e_i == ti[kk:kk + 1, :] for kk in range(TOP_K)]
    sel = hits[0] | hits[1] | hits[2] | hits[3]
    s_i = lax.broadcasted_iota(I32, (tm, tm), 0)
    t_i = lax.broadcasted_iota(I32, (tm, tm), 1)
    before = (s_i < t_i).astype(BF16)
    cum = _dot(sel.astype(BF16), before) + off_ref[0]
    k8 = lax.broadcasted_iota(I32, (SUBLANES, tm), 0)
    pos = jnp.zeros((SUBLANES, tm), F32)
    for kk in range(TOP_K):
        pk = jnp.sum(jnp.where(hits[kk], cum, 0.0), axis=0, keepdims=True)
        pos = jnp.where(k8 == kk, pk, pos)
    pos_ref[...] = pos.astype(I32)


def _positions(topi, blk_off, e_pad):
    n = topi.shape[1]
    tm = POS_TM
    return pl.pallas_call(
        functools.partial(_pos_kernel, e_pad=e_pad),
        grid=(n // tm,),
        in_specs=[pl.BlockSpec((SUBLANES, tm), lambda i: (0, i)),
                  pl.BlockSpec((1, e_pad, 1), lambda i: (i, 0, 0))],
        out_specs=pl.BlockSpec((SUBLANES, tm), lambda i: (0, i)),
        out_shape=jax.ShapeDtypeStruct((SUBLANES, n), I32),
        compiler_params=_params(1),
    )(topi, blk_off)


def _blocked_pos(pos, tm):
    n = pos.shape[1]
    p = pos[:TOP_K].reshape(TOP_K, n // tm, tm)
    return jnp.transpose(p, (1, 0, 2)).reshape(n // tm, 1, TOP_K * tm)


def _dispatch_kernel(pos_ref, x_ref, init_ref, xs_ref, sem):
    del init_ref
    tm = x_ref.shape[0]

    def issue(t, carry):
        for kk in range(TOP_K):
            p = pos_ref[0, 0, kk * tm + t]
            pltpu.make_async_copy(x_ref.at[pl.ds(t, 1)], xs_ref.at[pl.ds(p, 1)], sem).start()
        return carry

    lax.fori_loop(0, tm, issue, 0)
    for _ in range(TOP_K):
        pltpu.make_async_copy(x_ref, xs_ref.at[pl.ds(0, tm)], sem).wait()


def _dispatch(pos_blocked, x1p, n_rows):
    n, w = x1p.shape
    tm = DISPATCH_TM
    return pl.pallas_call(
        _dispatch_kernel,
        grid=(n // tm,),
        in_specs=[pl.BlockSpec((1, 1, TOP_K * tm), lambda i: (i, 0, 0), memory_space=pltpu.SMEM),
                  pl.BlockSpec((tm, w), lambda i: (i, 0)),
                  pl.BlockSpec(memory_space=pl.ANY)],
        out_specs=pl.BlockSpec(memory_space=pl.ANY),
        out_shape=jax.ShapeDtypeStruct((n_rows, w), U32),
        scratch_shapes=[pltpu.SemaphoreType.DMA(())],
        input_output_aliases={2: 0},
        compiler_params=_params(1),
    )(pos_blocked, x1p, jnp.zeros((n_rows, w), U32))


def _expert_kernel(be_ref, nu_ref, xs_ref, wg_ref, wu_ref, bg_ref, bu_ref, wd_ref, bd_ref, ys_ref, xb_ref, acc_ref):
    del be_ref
    i, j = pl.program_id(0), pl.program_id(1)
    nj = pl.num_programs(1)

    @pl.when(i < nu_ref[0])
    def _():
        @pl.when(j == 0)
        def _():
            lo, hi = _unpack_halves(xs_ref[...])
            xb_ref[...] = jnp.concatenate([lo.astype(BF16), hi.astype(BF16)], axis=1)

        xb = xb_ref[...]
        g = jnp.minimum(_dot(xb, wg_ref[...]) + bg_ref[...], SWIGLU_LIMIT)
        u = jnp.clip(_dot(xb, wu_ref[...]) + bu_ref[...], -SWIGLU_LIMIT, SWIGLU_LIMIT)
        act = (u + 1.0) * (g * jax.nn.sigmoid(SWIGLU_ALPHA * g))
        part = _dot(act.astype(BF16), wd_ref[...])

        @pl.when(j == 0)
        def _():
            acc_ref[...] = part

        @pl.when(j > 0)
        def _():
            acc_ref[...] += part

        @pl.when(j == nj - 1)
        def _():
            ys_ref[...] = _pack_halves(acc_ref[...] + bd_ref[...])

    @pl.when((i >= nu_ref[0]) & (j == nj - 1))
    def _():
        ys_ref[...] = jnp.zeros_like(ys_ref)


def _experts(blk_e, n_used, xs, w_gu, b_gu, w_dn, b_dn):
    n_rows, w = xs.shape
    e, d, f2 = w_gu.shape
    f = f2 // 2
    tb, tf = EXPERT_ROWS, EXPERT_TF
    nj = f // tf
    nb = n_rows // tb

    def blk(i, nu):
        return jnp.minimum(i, nu[0] - 1)

    def col(i, j, nu):
        return jnp.where(i < nu[0], j, nj - 1)

    grid_spec = pltpu.PrefetchScalarGridSpec(
        num_scalar_prefetch=2,
        grid=(nb, nj),
        in_specs=[pl.BlockSpec((tb, w), lambda i, j, be, nu: (blk(i, nu), 0)),
                  pl.BlockSpec((None, d, tf), lambda i, j, be, nu: (be[blk(i, nu)], 0, col(i, j, nu))),
                  pl.BlockSpec((None, d, tf), lambda i, j, be, nu: (be[blk(i, nu)], 0, nj + col(i, j, nu))),
                  pl.BlockSpec((None, 1, tf), lambda i, j, be, nu: (be[blk(i, nu)], 0, col(i, j, nu))),
                  pl.BlockSpec((None, 1, tf), lambda i, j, be, nu: (be[blk(i, nu)], 0, nj + col(i, j, nu))),
                  pl.BlockSpec((None, tf, d), lambda i, j, be, nu: (be[blk(i, nu)], col(i, j, nu), 0)),
                  pl.BlockSpec((None, 1, d), lambda i, j, be, nu: (be[blk(i, nu)], 0, 0))],
        out_specs=pl.BlockSpec((tb, w), lambda i, j, be, nu: (i, 0)),
        scratch_shapes=[pltpu.VMEM((tb, d), BF16), pltpu.VMEM((tb, d), F32)],
    )
    return pl.pallas_call(
        _expert_kernel,
        grid_spec=grid_spec,
        out_shape=jax.ShapeDtypeStruct((n_rows, w), U32),
        compiler_params=_params(2),
    )(blk_e, n_used, xs, w_gu, w_gu, b_gu.reshape(e, 1, f2), b_gu.reshape(e, 1, f2), w_dn, b_dn.reshape(e, 1, d))


def _combine_kernel(pos_ref, ys_ref, x1_ref, gcol_ref, p_ref, g_ref, b_ref, wpg_ref, wpp_ref, out_ref, buf_ref, sem,
                    *, alpha):
    tm = x1_ref.shape[0]

    def issue(t, carry):
        for kk in range(TOP_K):
            r = pos_ref[0, 0, kk * tm + t]
            pltpu.make_async_copy(ys_ref.at[pl.ds(r, 1)], buf_ref.at[kk, pl.ds(t, 1)], sem).start()
        return carry

    lax.fori_loop(0, tm, issue, 0)
    for kk in range(TOP_K):
        pltpu.make_async_copy(ys_ref.at[pl.ds(0, tm)], buf_ref.at[kk], sem).wait()

    gates = gcol_ref[...]
    lo = hi = None
    for kk in range(TOP_K):
        l, h = _unpack_halves(buf_ref[kk])
        gk = gates[:, kk:kk + 1]
        lo = gk * l if lo is None else lo + gk * l
        hi = gk * h if hi is None else hi + gk * h
    ffn = jnp.concatenate([lo, hi], axis=1)
    x2 = _layer_norm(alpha * x1_ref[...] + ffn, g_ref[...], b_ref[...])
    gate = jax.nn.sigmoid(_dot(x2.astype(BF16), wpg_ref[...]))
    out_ref[...] = x2 + gate * _dot(p_ref[...].astype(BF16), wpp_ref[...])


def _combine(pos_blocked, ys, x1, gcol, p, ln_g, ln_b, w_pg, w_pp, alpha):
    n, d = x1.shape
    tm = COMBINE_TM
    pd = p.shape[1]
    full = lambda i: (0, 0)
    return pl.pallas_call(
        functools.partial(_combine_kernel, alpha=alpha),
        grid=(n // tm,),
        in_specs=[pl.BlockSpec((1, 1, TOP_K * tm), lambda i: (i, 0, 0), memory_space=pltpu.SMEM),
                  pl.BlockSpec(memory_space=pl.ANY),
                  pl.BlockSpec((tm, d), lambda i: (i, 0)),
                  pl.BlockSpec((tm, LANES), lambda i: (i, 0)),
                  pl.BlockSpec((tm, pd), lambda i: (i, 0)),
                  pl.BlockSpec((1, d), full),
                  pl.BlockSpec((1, d), full),
                  pl.BlockSpec((d, d), full),
                  pl.BlockSpec((pd, d), full)],
        out_specs=pl.BlockSpec((tm, d), lambda i: (i, 0)),
        out_shape=jax.ShapeDtypeStruct((n, d), F32),
        scratch_shapes=[pltpu.VMEM((TOP_K, tm, d // 2), U32), pltpu.SemaphoreType.DMA(())],
        compiler_params=_params(1),
    )(pos_blocked, ys, x1, gcol, p, ln_g.reshape(1, d), ln_b.reshape(1, d), w_pg, w_pp)


def _moe_tail(x1, x1p, topi, gcol, cnt, p, ln_g, ln_b, w_gu, b_gu, w_dn, b_dn, w_pg, w_pp, alpha):
    n, d = x1.shape
    e = w_gu.shape[0]
    e_pad = cnt.shape[1]
    tb = EXPERT_ROWS
    c = cnt[:, :, 0].astype(I32)
    c = c.reshape(n // POS_TM, POS_TM // POST_TM, e_pad).sum(axis=1)
    total = c.sum(axis=0)
    padded = (total + tb - 1) // tb * tb
    pad_end = jnp.cumsum(padded)
    pad_start = pad_end - padded
    blk_off = (pad_start[None, :] + jnp.cumsum(c, axis=0) - c).astype(F32)[:, :, None]
    n_rows = -(-n * TOP_K // tb) * tb + e * tb
    nb = n_rows // tb
    blk_e = jnp.clip(jnp.searchsorted(pad_end, jnp.arange(nb, dtype=I32) * tb, side="right"), 0, e - 1).astype(I32)
    n_used = (pad_end[-1] // tb).astype(I32).reshape(1)

    pos = _positions(topi, blk_off, e_pad)
    xs = _dispatch(_blocked_pos(pos, DISPATCH_TM), x1p, n_rows)
    ys = _experts(blk_e, n_used, xs, w_gu, b_gu, w_dn, b_dn)
    return _combine(_blocked_pos(pos, COMBINE_TM), ys, x1, gcol, p, ln_g, ln_b, w_pg, w_pp, alpha)


def kernel(x, p, ln_g, ln_b, w_in_a, b_gate_a, norm_a, w_out_a, w_in_b, conv_b, w_out_b, w_router, b_router,
           w_gu, b_gu, w_dn, b_dn, w_ple_gate, w_ple_proj):
    bsz, seq, d = x.shape
    depth = ln_g.shape[0]
    n = bsz * seq
    alpha = (2 * depth) ** 0.25
    heads = b_gate_a.shape[-1] // 2
    dv = d // heads
    dk = dv // 2
    qkvo = 2 * heads * dk + 2 * heads * dv
    xf = x.reshape(n, d)
    for i in range(depth):
        j = i // 2
        if i % 2 == 0:
            col, row = _mlstm_gates(xf, w_in_a[j][:, qkvo:], b_gate_a[j], heads)
            proj = _proj(xf, w_in_a[j][:, :qkvo].astype(BF16), PROJ_TM, PROJ_TN_A)
            a = _mlstm(proj, col, row, norm_a[j], bsz, seq, heads, dk, dv)
            w_out = w_out_a[j]
        else:
            a = _conv_front(xf, w_in_b[j].astype(BF16), conv_b[j], seq)
            w_out = w_out_b[j]
        x1, x1p, topi, gcol, cnt = _post_mixer(a, w_out.astype(BF16), xf, ln_g[i, 0], ln_b[i, 0],
                                               w_router[i], b_router[i], alpha)
        xf = _moe_tail(x1, x1p, topi, gcol, cnt, p[i].reshape(n, -1), ln_g[i, 1], ln_b[i, 1],
                       w_gu[i].astype(BF16), b_gu[i], w_dn[i].astype(BF16), b_dn[i],
                       w_ple_gate[i].astype(BF16), w_ple_proj[i].astype(BF16), alpha)
    return xf.reshape(bsz, seq, d)
```

```python
import functools

import jax
import jax.numpy as jnp
from jax import lax
from jax.experimental import pallas as pl
from jax.experimental.pallas import tpu as pltpu

F32 = jnp.float32
BF16 = jnp.bfloat16
U32 = jnp.uint32
I32 = jnp.int32

GATE_SOFTCAP = 15.0
CONV_WIDTH = 3
TOP_K = 4
SWIGLU_ALPHA = 1.702
SWIGLU_LIMIT = 7.0
LN_EPS = 1e-5
RMS_EPS = 1e-6

LANES = 128
SUBLANES = 8
MXU_COLS = 256
VMEM_LIMIT_BYTES = 60000 * 1024

MLSTM_ROWS = 256
PROJ_TM = 512
PROJ_TN_A = 1024
PROJ_TN_B = 512
POST_TM = 256
POS_TM = 512
DISPATCH_TM = 512
EXPERT_ROWS = 512
EXPERT_UP_COLS = 1024
COMBINE_TM = 256


def _dot(a, b):
    return jnp.dot(a, b, preferred_element_type=F32)


def _dot_nt(a, b):
    return lax.dot_general(a, b, (((1,), (1,)), ((), ())), preferred_element_type=F32)


def _dot_tn(a, b):
    return lax.dot_general(a, b, (((0,), (0,)), ((), ())), preferred_element_type=F32)


def _split2(x):
    hi = x.astype(BF16)
    lo = (x - hi.astype(F32)).astype(BF16)
    return hi, lo


def _split3(x):
    hi = x.astype(BF16)
    r = x - hi.astype(F32)
    mid = r.astype(BF16)
    lo = (r - mid.astype(F32)).astype(BF16)
    return hi, mid, lo


def _pack_halves(y):
    half = y.shape[1] // 2
    lo = lax.bitcast_convert_type(y[:, :half].astype(BF16).astype(F32), U32) >> 16
    hi = lax.bitcast_convert_type(y[:, half:].astype(BF16).astype(F32), U32) & jnp.uint32(0xFFFF0000)
    return lo | hi


def _unpack_halves(u):
    lo = lax.bitcast_convert_type(u << 16, F32)
    hi = lax.bitcast_convert_type(u & jnp.uint32(0xFFFF0000), F32)
    return lo, hi


def _layer_norm(y, g, b):
    mu = jnp.mean(y, axis=-1, keepdims=True)
    yc = y - mu
    var = jnp.mean(yc * yc, axis=-1, keepdims=True)
    return yc * lax.rsqrt(var + LN_EPS) * g + b


def _params(n_axes):
    return pltpu.CompilerParams(dimension_semantics=("arbitrary",) * n_axes,
                                vmem_limit_bytes=VMEM_LIMIT_BYTES)


def _gate_kernel(x_ref, w_ref, b_ref, col_ref, row_ref, *, heads):
    rows = x_ref.shape[0]
    xh, xl = _split2(x_ref[...])
    wh, wl = _split2(w_ref[...])
    g = _dot(xh, wh) + _dot(xh, wl) + _dot(xl, wh) + b_ref[...]
    g = GATE_SOFTCAP * jnp.tanh(g / GATE_SOFTCAP)
    lf = -jnp.log1p(jnp.exp(-g))
    r_i = lax.broadcasted_iota(I32, (rows, rows), 0)
    c_i = lax.broadcasted_iota(I32, (rows, rows), 1)
    tril = (c_i <= r_i).astype(BF16)
    p0, p1, p2 = _split3(lf)
    bsum = _dot(tril, p0) + _dot(tril, p1) + _dot(tril, p2)
    lane = lax.broadcasted_iota(I32, g.shape, 1)
    col = jnp.where(lane < heads, g, jnp.where(lane < 2 * heads, bsum, 0.0))
    col_ref[...] = col
    row_ref[...] = col.T[: row_ref.shape[0]]


def _mlstm_gates(xf, w_g, b_g, heads):
    n, d = xf.shape
    rows = MLSTM_ROWS
    r_pad = -(-2 * heads // SUBLANES) * SUBLANES
    wg = jnp.zeros((d, LANES), F32).at[:, : 2 * heads].set(w_g)
    bg = jnp.zeros((1, LANES), F32).at[0, : 2 * heads].set(b_g)
    return pl.pallas_call(
        functools.partial(_gate_kernel, heads=heads),
        grid=(n // rows,),
        in_specs=[pl.BlockSpec((rows, d), lambda i: (i, 0)),
                  pl.BlockSpec((d, LANES), lambda i: (0, 0)),
                  pl.BlockSpec((1, LANES), lambda i: (0, 0))],
        out_specs=[pl.BlockSpec((rows, LANES), lambda i: (i, 0)),
                   pl.BlockSpec((r_pad, rows), lambda i: (0, i))],
        out_shape=[jax.ShapeDtypeStruct((n, LANES), F32),
                   jax.ShapeDtypeStruct((r_pad, n), F32)],
        compiler_params=_params(1),
    )(xf, wg, bg)


def _proj_kernel(x_ref, w_ref, o_ref, xb_ref):
    @pl.when(pl.program_id(1) == 0)
    def _():
        xb_ref[...] = x_ref[...].astype(BF16)

    o_ref[...] = _dot(xb_ref[...], w_ref[...]).astype(o_ref.dtype)


def _proj(xf, w, tm, tn):
    n, d = xf.shape
    m = w.shape[1]
    while m % tn:
        tn //= 2
    return pl.pallas_call(
        _proj_kernel,
        grid=(n // tm, m // tn),
        in_specs=[pl.BlockSpec((tm, d), lambda i, j: (i, 0)),
                  pl.BlockSpec((d, tn), lambda i, j: (0, j))],
        out_specs=pl.BlockSpec((tm, tn), lambda i, j: (i, j)),
        out_shape=jax.ShapeDtypeStruct((n, m), BF16),
        scratch_shapes=[pltpu.VMEM((tm, d), BF16)],
        compiler_params=_params(2),
    )(xf, w)


def _mlstm_kernel(q_ref, k_ref, v_ref, o_ref, col_ref, row_ref, norm_ref, out_ref, c_ref, n_ref, *, heads, dk, dv):
    rows = q_ref.shape[0]
    scale = dk ** -0.5

    @pl.when(pl.program_id(1) == 0)
    def _():
        c_ref[...] = jnp.zeros_like(c_ref)
        n_ref[...] = jnp.zeros_like(n_ref)

    col = col_ref[...]
    row = row_ref[...]
    t_i = lax.broadcasted_iota(I32, (rows, rows), 0)
    s_i = lax.broadcasted_iota(I32, (rows, rows), 1)
    causal = s_i <= t_i
    for h in range(heads):
        q = q_ref[:, h * dk:(h + 1) * dk]
        k = k_ref[:, h * dk:(h + 1) * dk]
        v = v_ref[:, h * dv:(h + 1) * dv]
        li_c = col[:, h:h + 1]
        b_c = col[:, heads + h:heads + h + 1]
        li_r = row[h:h + 1, :]
        b_r = row[heads + h:heads + h + 1, :]
        b_last = b_c[rows - 1:rows, :]
        decay_w = jnp.where(causal, jnp.exp(b_c - b_r + li_r), 0.0)
        sw = _dot_nt(q, k) * scale * decay_w
        eb = jnp.exp(b_c) * scale
        c_prev = c_ref[h]
        n_prev = n_ref[h]
        num = _dot(sw.astype(BF16), v) + eb * _dot(q, c_prev.astype(BF16))
        qn = jnp.sum(q.astype(F32) * n_prev, axis=-1, keepdims=True)
        den = jnp.sum(sw, axis=-1, keepdims=True) + eb * qn
        hh = num / jnp.maximum(jnp.abs(den), 1.0)
        wk = jnp.exp(b_last - b_c + li_c)
        carry = jnp.exp(b_last)
        kf = k.astype(F32) * wk
        c_ref[h] = carry * c_prev + _dot_tn(kf.astype(BF16), v)
        n_ref[h] = carry * n_prev + jnp.sum(kf, axis=0, keepdims=True)
        hh = hh * lax.rsqrt(jnp.mean(hh * hh, axis=-1, keepdims=True) + RMS_EPS)
        hh = hh * norm_ref[:, h * dv:(h + 1) * dv]
        og = o_ref[:, h * dv:(h + 1) * dv].astype(F32)
        out_ref[:, h * dv:(h + 1) * dv] = (hh * jax.nn.sigmoid(og)).astype(out_ref.dtype)


def _mlstm(proj, col, row, norm, bsz, seq, heads, dk, dv):
    rows = MLSTM_ROWS
    nc = seq // rows
    qk_w, v_w = heads * dk, heads * dv
    assert v_w == 2 * qk_w
    r_pad = row.shape[0]
    tok = lambda b, c: b * nc + c
    return pl.pallas_call(
        functools.partial(_mlstm_kernel, heads=heads, dk=dk, dv=dv),
        grid=(bsz, nc),
        in_specs=[pl.BlockSpec((rows, qk_w), lambda b, c: (tok(b, c), 0)),
                  pl.BlockSpec((rows, qk_w), lambda b, c: (tok(b, c), 1)),
                  pl.BlockSpec((rows, v_w), lambda b, c: (tok(b, c), 1)),
                  pl.BlockSpec((rows, v_w), lambda b, c: (tok(b, c), 2)),
                  pl.BlockSpec((rows, LANES), lambda b, c: (tok(b, c), 0)),
                  pl.BlockSpec((r_pad, rows), lambda b, c: (0, tok(b, c))),
                  pl.BlockSpec((1, v_w), lambda b, c: (0, 0))],
        out_specs=pl.BlockSpec((rows, v_w), lambda b, c: (tok(b, c), 0)),
        out_shape=jax.ShapeDtypeStruct((bsz * seq, v_w), BF16),
        scratch_shapes=[pltpu.VMEM((heads, dk, dv), F32), pltpu.VMEM((heads, 1, dk), F32)],
        compiler_params=_params(2),
    )(proj, proj, proj, proj, col, row, norm.reshape(1, v_w))


def _conv_kernel(x_ref, wb_ref, wc_ref, wu_ref, cw_ref, o_ref, xb_ref, carry_ref, *, steps_per_seq):
    i, j = pl.program_id(0), pl.program_id(1)
    tm = x_ref.shape[0]

    @pl.when(j == 0)
    def _():
        xb_ref[...] = x_ref[...].astype(BF16)

    xb = xb_ref[...]
    z = _dot(xb, wc_ref[...]) * _dot(xb, wu_ref[...])
    @pl.when(i % steps_per_seq == 0)
    def _():
        carry_ref[j] = jnp.zeros(carry_ref.shape[1:], F32)

    prev = carry_ref[j]
    carry_ref[j] = z[tm - SUBLANES:, :]
    r_i = lax.broadcasted_iota(I32, z.shape, 0)
    z1 = jnp.where(r_i == 0, prev[SUBLANES - 1:SUBLANES, :], pltpu.roll(z, 1, 0))
    z2 = jnp.where(r_i == 0, prev[SUBLANES - 2:SUBLANES - 1, :],
                   jnp.where(r_i == 1, prev[SUBLANES - 1:SUBLANES, :], pltpu.roll(z, 2, 0)))
    cw = cw_ref[...]
    zc = cw[0:1, :] * z2 + cw[1:2, :] * z1 + cw[2:3, :] * z
    o_ref[...] = (_dot(xb, wb_ref[...]) * zc).astype(o_ref.dtype)


def _conv_front(xf, w_in, conv_w, seq):
    n, d = xf.shape
    tm, tn = PROJ_TM, PROJ_TN_B
    nj = d // tn
    cw = jnp.zeros((SUBLANES, d), F32).at[:CONV_WIDTH].set(conv_w)
    return pl.pallas_call(
        functools.partial(_conv_kernel, steps_per_seq=seq // tm),
        grid=(n // tm, nj),
        in_specs=[pl.BlockSpec((tm, d), lambda i, j: (i, 0)),
                  pl.BlockSpec((d, tn), lambda i, j: (0, j)),
                  pl.BlockSpec((d, tn), lambda i, j: (0, nj + j)),
                  pl.BlockSpec((d, tn), lambda i, j: (0, 2 * nj + j)),
                  pl.BlockSpec((SUBLANES, tn), lambda i, j: (0, j))],
        out_specs=pl.BlockSpec((tm, tn), lambda i, j: (i, j)),
        out_shape=jax.ShapeDtypeStruct((n, d), BF16),
        scratch_shapes=[pltpu.VMEM((tm, d), BF16), pltpu.VMEM((nj, SUBLANES, tn), F32)],
        compiler_params=_params(2),
    )(xf, w_in, w_in, w_in, cw)


def _post_kernel(a_ref, w_ref, x_ref, g_ref, b_ref, wr_ref, br_ref,
                 x1_ref, x1p_ref, topi_ref, gcol_ref, cnt_ref, *, alpha, e_pad):
    tm = a_ref.shape[0]
    y = alpha * x_ref[...] + _dot(a_ref[...], w_ref[...])
    x1 = _layer_norm(y, g_ref[...], b_ref[...])
    x1_ref[...] = x1
    x1p_ref[...] = _pack_halves(x1)
    xh, xl = _split2(x1)
    w2 = wr_ref[...]
    l2 = _dot(xh, w2)
    logits = l2[:, :LANES] + l2[:, LANES:] + _dot(xl, w2[:, :LANES]) + br_ref[...]
    lt = logits.T[:e_pad]
    e_i = lax.broadcasted_iota(I32, lt.shape, 0)
    cur = lt
    ids, vals = [], []
    for _ in range(TOP_K):
        m = jnp.max(cur, axis=0, keepdims=True)
        idx = jnp.min(jnp.where(cur == m, e_i, e_pad), axis=0, keepdims=True)
        cur = jnp.where(e_i == idx, -jnp.inf, cur)
        ids.append(idx)
        vals.append(m)
    ex = [jnp.exp(v - vals[0]) for v in vals]
    den = ex[0] + ex[1] + ex[2] + ex[3]
    k8 = lax.broadcasted_iota(I32, (SUBLANES, tm), 0)
    k128 = lax.broadcasted_iota(I32, (LANES, tm), 0)
    topi = jnp.zeros((SUBLANES, tm), I32)
    gmat = jnp.zeros((LANES, tm), F32)
    for kk in range(TOP_K):
        topi = jnp.where(k8 == kk, ids[kk], topi)
        gmat = jnp.where(k128 == kk, ex[kk] / den, gmat)
    topi_ref[...] = topi
    gcol_ref[...] = gmat.T
    sel = (cur == -jnp.inf).astype(F32)
    cnt_ref[0] = jnp.broadcast_to(jnp.sum(sel, axis=1, keepdims=True), (e_pad, LANES))


def _post_mixer(a, w_out, xf, ln_g, ln_b, w_router, b_router, alpha):
    n, d = xf.shape
    tm = POST_TM
    e = w_router.shape[1]
    assert TOP_K <= e <= LANES
    e_pad = -(-e // SUBLANES) * SUBLANES
    wr = jnp.zeros((d, LANES), F32).at[:, :e].set(w_router)
    wr_hi = wr.astype(BF16)
    wr_lo = (wr - wr_hi.astype(F32)).astype(BF16)
    wr2 = jnp.concatenate([wr_hi, wr_lo], axis=1)
    br = jnp.full((1, LANES), -1e30, F32).at[0, :e].set(b_router)
    full = lambda i: (0, 0)
    return pl.pallas_call(
        functools.partial(_post_kernel, alpha=alpha, e_pad=e_pad),
        grid=(n // tm,),
        in_specs=[pl.BlockSpec((tm, d), lambda i: (i, 0)),
                  pl.BlockSpec((d, d), full),
                  pl.BlockSpec((tm, d), lambda i: (i, 0)),
                  pl.BlockSpec((1, d), full),
                  pl.BlockSpec((1, d), full),
                  pl.BlockSpec((d, 2 * LANES), full),
                  pl.BlockSpec((1, LANES), full)],
        out_specs=[pl.BlockSpec((tm, d), lambda i: (i, 0)),
                   pl.BlockSpec((tm, d // 2), lambda i: (i, 0)),
                   pl.BlockSpec((SUBLANES, tm), lambda i: (0, i)),
                   pl.BlockSpec((tm, LANES), lambda i: (i, 0)),
                   pl.BlockSpec((1, e_pad, LANES), lambda i: (i, 0, 0))],
        out_shape=[jax.ShapeDtypeStruct((n, d), F32),
                   jax.ShapeDtypeStruct((n, d // 2), U32),
                   jax.ShapeDtypeStruct((SUBLANES, n), I32),
                   jax.ShapeDtypeStruct((n, LANES), F32),
                   jax.ShapeDtypeStruct((n // tm, e_pad, LANES), F32)],
        compiler_params=_params(1),
    )(a, w_out, xf, ln_g.reshape(1, d), ln_b.reshape(1, d), wr2, br)


def _pos_kernel(topi_ref, off_ref, pos_ref, *, e_pad):
    tm = topi_ref.shape[1]
    ti = topi_ref[...]
    e_i = lax.broadcasted_iota(I32, (e_pad, tm), 0)
    hits = [e_i == ti[kk:kk + 1, :] for kk in range(TOP_K)]
    sel = hits[0] | hits[1] | hits[2] | hits[3]
    s_i = lax.broadcasted_iota(I32, (tm, tm), 0)
    t_i = lax.broadcasted_iota(I32, (tm, tm), 1)
    before = (s_i < t_i).astype(BF16)
    cum = _dot(sel.astype(BF16), before) + off_ref[0]
    k8 = lax.broadcasted_iota(I32, (SUBLANES, tm), 0)
    pos = jnp.zeros((SUBLANES, tm), F32)
    for kk in range(TOP_K):
        pk = jnp.sum(jnp.where(hits[kk], cum, 0.0), axis=0, keepdims=True)
        pos = jnp.where(k8 == kk, pk, pos)
    pos_ref[...] = pos.astype(I32)


def _positions(topi, blk_off, e_pad):
    n = topi.shape[1]
    tm = POS_TM
    return pl.pallas_call(
        functools.partial(_pos_kernel, e_pad=e_pad),
        grid=(n // tm,),
        in_specs=[pl.BlockSpec((SUBLANES, tm), lambda i: (0, i)),
                  pl.BlockSpec((1, e_pad, 1), lambda i: (i, 0, 0))],
        out_specs=pl.BlockSpec((SUBLANES, tm), lambda i: (0, i)),
        out_shape=jax.ShapeDtypeStruct((SUBLANES, n), I32),
        compiler_params=_params(1),
    )(topi, blk_off)


def _blocked_pos(pos, tm):
    n = pos.shape[1]
    p = pos[:TOP_K].reshape(TOP_K, n // tm, tm)
    return jnp.transpose(p, (1, 0, 2)).reshape(n // tm, 1, TOP_K * tm)


def _dispatch_kernel(fill_ref, pos_ref, x_ref, xs_ref, zero_ref, sem, fill_sem):
    tm = x_ref.shape[0]
    tb = zero_ref.shape[0]

    @pl.when(pl.program_id(0) == 0)
    def _():
        zero_ref[...] = jnp.zeros_like(zero_ref)

        def each_copy(b, action):
            cnt = fill_ref[b]
            first = (b + 1) * tb - cnt
            for r in range(SUBLANES - 1):
                @pl.when(r < cnt % SUBLANES)
                def _(r=r):
                    action(pltpu.make_async_copy(zero_ref.at[pl.ds(0, 1)], xs_ref.at[pl.ds(first + r, 1)], fill_sem))

            groups = cnt // SUBLANES
            row = (b + 1) * tb - groups * SUBLANES
            bit = tb // SUBLANES
            while bit:
                @pl.when((groups & bit) != 0)
                def _(row=row, bit=bit):
                    rows = bit * SUBLANES
                    dst = xs_ref.at[pl.ds(pl.multiple_of(row, SUBLANES), rows)]
                    action(pltpu.make_async_copy(zero_ref.at[pl.ds(0, rows)], dst, fill_sem))

                row = row + (groups & bit) * SUBLANES
                bit //= 2

        def start_all(b, carry):
            each_copy(b, lambda cp: cp.start())
            return carry

        def wait_all(b, carry):
            each_copy(b, lambda cp: cp.wait())
            return carry

        lax.fori_loop(0, fill_ref.shape[0], start_all, 0)
        lax.fori_loop(0, fill_ref.shape[0], wait_all, 0)

    def issue(t, carry):
        for kk in range(TOP_K):
            p = pos_ref[0, 0, kk * tm + t]
            pltpu.make_async_copy(x_ref.at[pl.ds(t, 1)], xs_ref.at[pl.ds(p, 1)], sem).start()
        return carry

    lax.fori_loop(0, tm, issue, 0)
    for _ in range(TOP_K):
        pltpu.make_async_copy(x_ref, xs_ref.at[pl.ds(0, tm)], sem).wait()


def _dispatch(blk_fill, pos_blocked, x1p, n_rows):
    n, w = x1p.shape
    tm = DISPATCH_TM
    grid_spec = pltpu.PrefetchScalarGridSpec(
        num_scalar_prefetch=1,
        grid=(n // tm,),
        in_specs=[pl.BlockSpec((1, 1, TOP_K * tm), lambda i, fl: (i, 0, 0), memory_space=pltpu.SMEM),
                  pl.BlockSpec((tm, w), lambda i, fl: (i, 0))],
        out_specs=pl.BlockSpec(memory_space=pl.ANY),
        scratch_shapes=[pltpu.VMEM((EXPERT_ROWS, w), U32), pltpu.SemaphoreType.DMA(()), pltpu.SemaphoreType.DMA(())],
    )
    return pl.pallas_call(
        _dispatch_kernel,
        grid_spec=grid_spec,
        out_shape=jax.ShapeDtypeStruct((n_rows, w), U32),
        compiler_params=_params(1),
    )(blk_fill, pos_blocked, x1p)


def _expert_up_kernel(blk_ref, col_ref, exp_ref, flag_ref, xs_ref, wg_ref, wu_ref, bg_ref, bu_ref, act_ref,
                      wgb_ref, wub_ref):
    del blk_ref, col_ref, exp_ref
    flag = flag_ref[pl.program_id(0)]

    @pl.when((flag & 2) != 0)
    def _():
        wgb_ref[...] = wg_ref[...].astype(BF16)
        wub_ref[...] = wu_ref[...].astype(BF16)

    @pl.when((flag & 1) != 0)
    def _():
        lo, hi = _unpack_halves(xs_ref[...])
        xb = jnp.concatenate([lo.astype(BF16), hi.astype(BF16)], axis=1)
        for c in range(0, act_ref.shape[1], MXU_COLS):
            cs = slice(c, c + MXU_COLS)
            g = jnp.minimum(_dot(xb, wgb_ref[:, cs]) + bg_ref[:, cs], SWIGLU_LIMIT)
            u = jnp.clip(_dot(xb, wub_ref[:, cs]) + bu_ref[:, cs], -SWIGLU_LIMIT, SWIGLU_LIMIT)
            act_ref[:, cs] = ((u + 1.0) * (g * jax.nn.sigmoid(SWIGLU_ALPHA * g))).astype(act_ref.dtype)

    @pl.when((flag & 1) == 0)
    def _():
        act_ref[...] = jnp.zeros_like(act_ref)


def _expert_down_kernel(be_ref, nu_ref, first_ref, act_ref, wd_ref, bd_ref, ys_ref, wdb_ref):
    del be_ref
    i = pl.program_id(0)

    @pl.when(first_ref[i] != 0)
    def _():
        wdb_ref[...] = wd_ref[...].astype(BF16)

    @pl.when(i < nu_ref[0])
    def _():
        act = act_ref[...]
        half = ys_ref.shape[1]
        for c in range(0, half, MXU_COLS):
            lo = _dot(act, wdb_ref[:, c:c + MXU_COLS]) + bd_ref[:, c:c + MXU_COLS]
            hi = _dot(act, wdb_ref[:, half + c:half + c + MXU_COLS]) + bd_ref[:, half + c:half + c + MXU_COLS]
            ys_ref[:, c:c + MXU_COLS] = _pack_halves(jnp.concatenate([lo, hi], axis=1))

    @pl.when(i >= nu_ref[0])
    def _():
        ys_ref[...] = jnp.zeros_like(ys_ref)


def _experts(layer, sched, xs, w_gu, b_gu, w_dn, b_dn):
    n_rows, w = xs.shape
    _, e, d, f2 = w_gu.shape
    f = f2 // 2
    tb, fh = EXPERT_ROWS, EXPERT_UP_COLS
    nh = f // fh
    nb = n_rows // tb
    up_spec = pltpu.PrefetchScalarGridSpec(
        num_scalar_prefetch=4,
        grid=(nh * nb,),
        in_specs=[pl.BlockSpec((tb, w), lambda t, bl, co, ex, fl: (bl[t], 0)),
                  pl.BlockSpec((None, None, d, fh), lambda t, bl, co, ex, fl: (layer, ex[t], 0, co[t])),
                  pl.BlockSpec((None, None, d, fh), lambda t, bl, co, ex, fl: (layer, ex[t], 0, nh + co[t])),
                  pl.BlockSpec((None, None, 1, fh), lambda t, bl, co, ex, fl: (layer, ex[t], 0, co[t])),
                  pl.BlockSpec((None, None, 1, fh), lambda t, bl, co, ex, fl: (layer, ex[t], 0, nh + co[t]))],
        out_specs=pl.BlockSpec((tb, fh), lambda t, bl, co, ex, fl: (bl[t], co[t])),
        scratch_shapes=[pltpu.VMEM((d, fh), BF16), pltpu.VMEM((d, fh), BF16)],
    )
    b_gu4 = b_gu.reshape(b_gu.shape[0], e, 1, f2)
    act = pl.pallas_call(
        _expert_up_kernel,
        grid_spec=up_spec,
        out_shape=jax.ShapeDtypeStruct((n_rows, f), BF16),
        compiler_params=_params(1),
    )(sched["it_blk"], sched["it_col"], sched["it_exp"], sched["it_flag"], xs, w_gu, w_gu, b_gu4, b_gu4)

    def blk(i, nu):
        return jnp.minimum(i, nu[0] - 1)

    down_spec = pltpu.PrefetchScalarGridSpec(
        num_scalar_prefetch=3,
        grid=(nb,),
        in_specs=[pl.BlockSpec((tb, f), lambda i, be, nu, fi: (blk(i, nu), 0)),
                  pl.BlockSpec((None, None, f, d), lambda i, be, nu, fi: (layer, be[i], 0, 0)),
                  pl.BlockSpec((None, None, 1, d), lambda i, be, nu, fi: (layer, be[i], 0, 0))],
        out_specs=pl.BlockSpec((tb, w), lambda i, be, nu, fi: (i, 0)),
        scratch_shapes=[pltpu.VMEM((f, d), BF16)],
    )
    return pl.pallas_call(
        _expert_down_kernel,
        grid_spec=down_spec,
        out_shape=jax.ShapeDtypeStruct((n_rows, w), U32),
        compiler_params=_params(1),
    )(sched["blk_e"], sched["n_used"], sched["blk_first"], act, w_dn, b_dn.reshape(b_dn.shape[0], e, 1, d))


def _expert_schedule(total, n_rows, e, f):
    tb = EXPERT_ROWS
    nh = f // EXPERT_UP_COLS
    nb = n_rows // tb
    padded = (total + tb - 1) // tb * tb
    pad_end = jnp.cumsum(padded)
    pad_start = pad_end - padded
    n_used = pad_end[-1] // tb
    blocks = jnp.arange(nb, dtype=I32)
    blk_e = jnp.minimum(jnp.sum(pad_end[None, :] <= (blocks * tb)[:, None], axis=1), e - 1).astype(I32)
    blk_e = jnp.where(blocks < n_used, blk_e, blk_e[n_used - 1])
    blk_first = jnp.concatenate([jnp.ones((1,), I32), (blk_e[1:] != blk_e[:-1]).astype(I32)])
    grp_last = blocks == pad_end[blk_e] // tb - 1
    blk_fill = jnp.where(blocks < n_used, jnp.where(grp_last, (padded - total)[blk_e], 0), tb).astype(I32)
    items = jnp.arange(nh * nb, dtype=I32)
    grp_end = nh * (pad_end // tb)
    it_e = jnp.minimum(jnp.sum(grp_end[None, :] <= items[:, None], axis=1), e - 1).astype(I32)
    grp_blocks = jnp.maximum((padded // tb)[it_e], 1)
    r = items - nh * (pad_start // tb)[it_e]
    valid = items < nh * n_used
    spare = items - nh * n_used
    it_blk = jnp.where(valid, (pad_start // tb)[it_e] + r % grp_blocks, n_used + spare // nh)
    it_col = jnp.where(valid, r // grp_blocks, spare % nh)
    it_exp = jnp.where(valid, it_e, blk_e[n_used - 1])
    it_flag = jnp.where(valid, 1 + 2 * (r % grp_blocks == 0).astype(I32), 0)
    it_flag = it_flag.at[0].set(it_flag[0] | 2)
    return dict(pad_start=pad_start, blk_fill=blk_fill, blk_e=blk_e, n_used=n_used.astype(I32).reshape(1), blk_first=blk_first,
                it_blk=it_blk.astype(I32), it_col=it_col.astype(I32), it_exp=it_exp.astype(I32),
                it_flag=it_flag.astype(I32))


def _combine_kernel(pos_ref, nxt_ref, ys_ref, x1_ref, gcol_ref, p_ref, g_ref, b_ref, wpg_ref, wpp_ref, out_ref,
                    buf_ref, sem, *, alpha):
    tm = x1_ref.shape[0]
    i = pl.program_id(0)
    slot = i % 2

    def gather(rows_ref, dst):
        def issue(t, carry):
            for kk in range(TOP_K):
                r = rows_ref[0, 0, kk * tm + t]
                pltpu.make_async_copy(ys_ref.at[pl.ds(r, 1)], buf_ref.at[dst, kk, pl.ds(t, 1)], sem.at[dst]).start()
            return carry

        lax.fori_loop(0, tm, issue, 0)

    @pl.when(i == 0)
    def _():
        gather(pos_ref, 0)

    @pl.when(i + 1 < pl.num_programs(0))
    def _():
        gather(nxt_ref, 1 - slot)

    for kk in range(TOP_K):
        pltpu.make_async_copy(ys_ref.at[pl.ds(0, tm)], buf_ref.at[slot, kk], sem.at[slot]).wait()

    gates = gcol_ref[...]
    lo = hi = None
    for kk in range(TOP_K):
        l, h = _unpack_halves(buf_ref[slot, kk])
        gk = gates[:, kk:kk + 1]
        lo = gk * l if lo is None else lo + gk * l
        hi = gk * h if hi is None else hi + gk * h
    ffn = jnp.concatenate([lo, hi], axis=1)
    x2 = _layer_norm(alpha * x1_ref[...] + ffn, g_ref[...], b_ref[...])
    gate = jax.nn.sigmoid(_dot(x2.astype(BF16), wpg_ref[...]))
    out_ref[...] = x2 + gate * _dot(p_ref[...].astype(BF16), wpp_ref[...])


def _combine(pos_blocked, ys, x1, gcol, p, ln_g, ln_b, w_pg, w_pp, alpha):
    n, d = x1.shape
    tm = COMBINE_TM
    pd = p.shape[1]
    full = lambda i: (0, 0)
    last = n // tm - 1
    return pl.pallas_call(
        functools.partial(_combine_kernel, alpha=alpha),
        grid=(n // tm,),
        in_specs=[pl.BlockSpec((1, 1, TOP_K * tm), lambda i: (i, 0, 0), memory_space=pltpu.SMEM),
                  pl.BlockSpec((1, 1, TOP_K * tm), lambda i: (jnp.minimum(i + 1, last), 0, 0),
                               memory_space=pltpu.SMEM),
                  pl.BlockSpec(memory_space=pl.ANY),
                  pl.BlockSpec((tm, d), lambda i: (i, 0)),
                  pl.BlockSpec((tm, LANES), lambda i: (i, 0)),
                  pl.BlockSpec((tm, pd), lambda i: (i, 0)),
                  pl.BlockSpec((1, d), full),
                  pl.BlockSpec((1, d), full),
                  pl.BlockSpec((d, d), full),
                  pl.BlockSpec((pd, d), full)],
        out_specs=pl.BlockSpec((tm, d), lambda i: (i, 0)),
        out_shape=jax.ShapeDtypeStruct((n, d), F32),
        scratch_shapes=[pltpu.VMEM((2, TOP_K, tm, d // 2), U32), pltpu.SemaphoreType.DMA((2,))],
        compiler_params=_params(1),
    )(pos_blocked, pos_blocked, ys, x1, gcol, p, ln_g.reshape(1, d), ln_b.reshape(1, d), w_pg, w_pp)


def _moe_tail(layer, x1, x1p, topi, gcol, cnt, p, ln_g, ln_b, w_gu, b_gu, w_dn, b_dn, w_pg, w_pp, alpha):
    n, d = x1.shape
    e = w_gu.shape[1]
    e_pad = cnt.shape[1]
    tb = EXPERT_ROWS
    c = cnt[:, :, 0].astype(I32)
    c = c.reshape(n // POS_TM, POS_TM // POST_TM, e_pad).sum(axis=1)
    n_rows = -(-n * TOP_K // tb) * tb + e * tb
    sched = _expert_schedule(c.sum(axis=0), n_rows, e, w_gu.shape[3] // 2)
    blk_off = (sched["pad_start"][None, :] + jnp.cumsum(c, axis=0) - c).astype(F32)[:, :, None]

    pos = _positions(topi, blk_off, e_pad)
    xs = _dispatch(sched["blk_fill"], _blocked_pos(pos, DISPATCH_TM), x1p, n_rows)
    ys = _experts(layer, sched, xs, w_gu, b_gu, w_dn, b_dn)
    return _combine(_blocked_pos(pos, COMBINE_TM), ys, x1, gcol, p, ln_g, ln_b, w_pg, w_pp, alpha)


def kernel(x, p, ln_g, ln_b, w_in_a, b_gate_a, norm_a, w_out_a, w_in_b, conv_b, w_out_b, w_router, b_router,
           w_gu, b_gu, w_dn, b_dn, w_ple_gate, w_ple_proj):
    bsz, seq, d = x.shape
    depth = ln_g.shape[0]
    n = bsz * seq
    alpha = (2 * depth) ** 0.25
    heads = b_gate_a.shape[-1] // 2
    dv = d // heads
    dk = dv // 2
    qkvo = 2 * heads * dk + 2 * heads * dv
    xf = x.reshape(n, d)
    for i in range(depth):
        j = i // 2
        if i % 2 == 0:
            col, row = _mlstm_gates(xf, w_in_a[j][:, qkvo:], b_gate_a[j], heads)
            proj = _proj(xf, w_in_a[j][:, :qkvo].astype(BF16), PROJ_TM, PROJ_TN_A)
            a = _mlstm(proj, col, row, norm_a[j], bsz, seq, heads, dk, dv)
            w_out = w_out_a[j]
        else:
            a = _conv_front(xf, w_in_b[j].astype(BF16), conv_b[j], seq)
            w_out = w_out_b[j]
        x1, x1p, topi, gcol, cnt = _post_mixer(a, w_out.astype(BF16), xf, ln_g[i, 0], ln_b[i, 0],
                                               w_router[i], b_router[i], alpha)
        xf = _moe_tail(i, x1, x1p, topi, gcol, cnt, p[i].reshape(n, -1), ln_g[i, 1], ln_b[i, 1],
                       w_gu, b_gu, w_dn, b_dn,
                       w_ple_gate[i].astype(BF16), w_ple_proj[i].astype(BF16), alpha)
    return xf.reshape(bsz, seq, d)
```

```python
import functools

import jax
import jax.numpy as jnp
from jax import lax
from jax.experimental import pallas as pl
from jax.experimental.pallas import tpu as pltpu

F32 = jnp.float32
BF16 = jnp.bfloat16
U32 = jnp.uint32
I32 = jnp.int32

GATE_SOFTCAP = 15.0
CONV_WIDTH = 3
TOP_K = 4
SWIGLU_ALPHA = 1.702
SWIGLU_LIMIT = 7.0
LN_EPS = 1e-5
RMS_EPS = 1e-6

LANES = 128
SUBLANES = 8
MXU_COLS = 256
VMEM_LIMIT_BYTES = 60000 * 1024

MLSTM_ROWS = 256
PROJ_TM = 512
PROJ_TN_A = 1024
PROJ_TN_B = 512
POST_TM = 256
POS_TM = 512
EXPERT_ROWS = 512
EXPERT_UP_COLS = 1024
COMBINE_TM = 256


def _dot(a, b):
    return jnp.dot(a, b, preferred_element_type=F32)


def _dot_nt(a, b):
    return lax.dot_general(a, b, (((1,), (1,)), ((), ())), preferred_element_type=F32)


def _dot_tn(a, b):
    return lax.dot_general(a, b, (((0,), (0,)), ((), ())), preferred_element_type=F32)


def _split2(x):
    hi = x.astype(BF16)
    lo = (x - hi.astype(F32)).astype(BF16)
    return hi, lo


def _split3(x):
    hi = x.astype(BF16)
    r = x - hi.astype(F32)
    mid = r.astype(BF16)
    lo = (r - mid.astype(F32)).astype(BF16)
    return hi, mid, lo


def _pack_halves(y):
    half = y.shape[1] // 2
    lo = lax.bitcast_convert_type(y[:, :half].astype(BF16).astype(F32), U32) >> 16
    hi = lax.bitcast_convert_type(y[:, half:].astype(BF16).astype(F32), U32) & jnp.uint32(0xFFFF0000)
    return lo | hi


def _unpack_halves(u):
    lo = lax.bitcast_convert_type(u << 16, F32)
    hi = lax.bitcast_convert_type(u & jnp.uint32(0xFFFF0000), F32)
    return lo, hi


def _layer_norm(y, g, b):
    mu = jnp.mean(y, axis=-1, keepdims=True)
    yc = y - mu
    var = jnp.mean(yc * yc, axis=-1, keepdims=True)
    return yc * lax.rsqrt(var + LN_EPS) * g + b


def _params(n_axes):
    return pltpu.CompilerParams(dimension_semantics=("arbitrary",) * n_axes,
                                vmem_limit_bytes=VMEM_LIMIT_BYTES)


def _gate_kernel(x_ref, w_ref, b_ref, col_ref, row_ref, *, heads):
    rows = x_ref.shape[0]
    xh, xl = _split2(x_ref[...])
    wh, wl = _split2(w_ref[...])
    g = _dot(xh, wh) + _dot(xh, wl) + _dot(xl, wh) + b_ref[...]
    g = GATE_SOFTCAP * jnp.tanh(g / GATE_SOFTCAP)
    lf = -jnp.log1p(jnp.exp(-g))
    r_i = lax.broadcasted_iota(I32, (rows, rows), 0)
    c_i = lax.broadcasted_iota(I32, (rows, rows), 1)
    tril = (c_i <= r_i).astype(BF16)
    p0, p1, p2 = _split3(lf)
    bsum = _dot(tril, p0) + _dot(tril, p1) + _dot(tril, p2)
    lane = lax.broadcasted_iota(I32, g.shape, 1)
    col = jnp.where(lane < heads, g, jnp.where(lane < 2 * heads, bsum, 0.0))
    col_ref[...] = col
    row_ref[...] = col.T[: row_ref.shape[0]]


def _mlstm_gates(xf, w_g, b_g, heads):
    n, d = xf.shape
    rows = MLSTM_ROWS
    r_pad = -(-2 * heads // SUBLANES) * SUBLANES
    wg = jnp.zeros((d, LANES), F32).at[:, : 2 * heads].set(w_g)
    bg = jnp.zeros((1, LANES), F32).at[0, : 2 * heads].set(b_g)
    return pl.pallas_call(
        functools.partial(_gate_kernel, heads=heads),
        grid=(n // rows,),
        in_specs=[pl.BlockSpec((rows, d), lambda i: (i, 0)),
                  pl.BlockSpec((d, LANES), lambda i: (0, 0)),
                  pl.BlockSpec((1, LANES), lambda i: (0, 0))],
        out_specs=[pl.BlockSpec((rows, LANES), lambda i: (i, 0)),
                   pl.BlockSpec((r_pad, rows), lambda i: (0, i))],
        out_shape=[jax.ShapeDtypeStruct((n, LANES), F32),
                   jax.ShapeDtypeStruct((r_pad, n), F32)],
        compiler_params=_params(1),
    )(xf, wg, bg)


def _proj_kernel(x_ref, w_ref, o_ref, xb_ref):
    @pl.when(pl.program_id(1) == 0)
    def _():
        xb_ref[...] = x_ref[...].astype(BF16)

    o_ref[...] = _dot(xb_ref[...], w_ref[...]).astype(o_ref.dtype)


def _proj(xf, w, tm, tn):
    n, d = xf.shape
    m = w.shape[1]
    while m % tn:
        tn //= 2
    return pl.pallas_call(
        _proj_kernel,
        grid=(n // tm, m // tn),
        in_specs=[pl.BlockSpec((tm, d), lambda i, j: (i, 0)),
                  pl.BlockSpec((d, tn), lambda i, j: (0, j))],
        out_specs=pl.BlockSpec((tm, tn), lambda i, j: (i, j)),
        out_shape=jax.ShapeDtypeStruct((n, m), BF16),
        scratch_shapes=[pltpu.VMEM((tm, d), BF16)],
        compiler_params=_params(2),
    )(xf, w)


def _mlstm_kernel(q_ref, k_ref, v_ref, o_ref, col_ref, row_ref, norm_ref, out_ref, c_ref, n_ref, *, heads, dk, dv):
    rows = q_ref.shape[0]
    scale = dk ** -0.5

    @pl.when(pl.program_id(1) == 0)
    def _():
        c_ref[...] = jnp.zeros_like(c_ref)
        n_ref[...] = jnp.zeros_like(n_ref)

    col = col_ref[...]
    row = row_ref[...]
    t_i = lax.broadcasted_iota(I32, (rows, rows), 0)
    s_i = lax.broadcasted_iota(I32, (rows, rows), 1)
    causal = s_i <= t_i
    for h in range(heads):
        q = q_ref[:, h * dk:(h + 1) * dk]
        k = k_ref[:, h * dk:(h + 1) * dk]
        v = v_ref[:, h * dv:(h + 1) * dv]
        li_c = col[:, h:h + 1]
        b_c = col[:, heads + h:heads + h + 1]
        li_r = row[h:h + 1, :]
        b_r = row[heads + h:heads + h + 1, :]
        b_last = b_c[rows - 1:rows, :]
        decay_w = jnp.where(causal, jnp.exp(b_c - b_r + li_r), 0.0)
        sw = _dot_nt(q, k) * scale * decay_w
        eb = jnp.exp(b_c) * scale
        c_prev = c_ref[h]
        n_prev = n_ref[h]
        num = _dot(sw.astype(BF16), v) + eb * _dot(q, c_prev.astype(BF16))
        qn = jnp.sum(q.astype(F32) * n_prev, axis=-1, keepdims=True)
        den = jnp.sum(sw, axis=-1, keepdims=True) + eb * qn
        hh = num / jnp.maximum(jnp.abs(den), 1.0)
        wk = jnp.exp(b_last - b_c + li_c)
        carry = jnp.exp(b_last)
        kf = k.astype(F32) * wk
        c_ref[h] = carry * c_prev + _dot_tn(kf.astype(BF16), v)
        n_ref[h] = carry * n_prev + jnp.sum(kf, axis=0, keepdims=True)
        hh = hh * lax.rsqrt(jnp.mean(hh * hh, axis=-1, keepdims=True) + RMS_EPS)
        hh = hh * norm_ref[:, h * dv:(h + 1) * dv]
        og = o_ref[:, h * dv:(h + 1) * dv].astype(F32)
        out_ref[:, h * dv:(h + 1) * dv] = (hh * jax.nn.sigmoid(og)).astype(out_ref.dtype)


def _mlstm(proj, col, row, norm, bsz, seq, heads, dk, dv):
    rows = MLSTM_ROWS
    nc = seq // rows
    qk_w, v_w = heads * dk, heads * dv
    assert v_w == 2 * qk_w
    r_pad = row.shape[0]
    tok = lambda b, c: b * nc + c
    return pl.pallas_call(
        functools.partial(_mlstm_kernel, heads=heads, dk=dk, dv=dv),
        grid=(bsz, nc),
        in_specs=[pl.BlockSpec((rows, qk_w), lambda b, c: (tok(b, c), 0)),
                  pl.BlockSpec((rows, qk_w), lambda b, c: (tok(b, c), 1)),
                  pl.BlockSpec((rows, v_w), lambda b, c: (tok(b, c), 1)),
                  pl.BlockSpec((rows, v_w), lambda b, c: (tok(b, c), 2)),
                  pl.BlockSpec((rows, LANES), lambda b, c: (tok(b, c), 0)),
                  pl.BlockSpec((r_pad, rows), lambda b, c: (0, tok(b, c))),
                  pl.BlockSpec((1, v_w), lambda b, c: (0, 0))],
        out_specs=pl.BlockSpec((rows, v_w), lambda b, c: (tok(b, c), 0)),
        out_shape=jax.ShapeDtypeStruct((bsz * seq, v_w), BF16),
        scratch_shapes=[pltpu.VMEM((heads, dk, dv), F32), pltpu.VMEM((heads, 1, dk), F32)],
        compiler_params=_params(2),
    )(proj, proj, proj, proj, col, row, norm.reshape(1, v_w))


def _conv_kernel(x_ref, wb_ref, wc_ref, wu_ref, cw_ref, o_ref, xb_ref, carry_ref, *, steps_per_seq):
    i, j = pl.program_id(0), pl.program_id(1)
    tm = x_ref.shape[0]

    @pl.when(j == 0)
    def _():
        xb_ref[...] = x_ref[...].astype(BF16)

    xb = xb_ref[...]
    z = _dot(xb, wc_ref[...]) * _dot(xb, wu_ref[...])
    @pl.when(i % steps_per_seq == 0)
    def _():
        carry_ref[j] = jnp.zeros(carry_ref.shape[1:], F32)

    prev = carry_ref[j]
    carry_ref[j] = z[tm - SUBLANES:, :]
    r_i = lax.broadcasted_iota(I32, z.shape, 0)
    z1 = jnp.where(r_i == 0, prev[SUBLANES - 1:SUBLANES, :], pltpu.roll(z, 1, 0))
    z2 = jnp.where(r_i == 0, prev[SUBLANES - 2:SUBLANES - 1, :],
                   jnp.where(r_i == 1, prev[SUBLANES - 1:SUBLANES, :], pltpu.roll(z, 2, 0)))
    cw = cw_ref[...]
    zc = cw[0:1, :] * z2 + cw[1:2, :] * z1 + cw[2:3, :] * z
    o_ref[...] = (_dot(xb, wb_ref[...]) * zc).astype(o_ref.dtype)


def _conv_front(xf, w_in, conv_w, seq):
    n, d = xf.shape
    tm, tn = PROJ_TM, PROJ_TN_B
    nj = d // tn
    cw = jnp.zeros((SUBLANES, d), F32).at[:CONV_WIDTH].set(conv_w)
    return pl.pallas_call(
        functools.partial(_conv_kernel, steps_per_seq=seq // tm),
        grid=(n // tm, nj),
        in_specs=[pl.BlockSpec((tm, d), lambda i, j: (i, 0)),
                  pl.BlockSpec((d, tn), lambda i, j: (0, j)),
                  pl.BlockSpec((d, tn), lambda i, j: (0, nj + j)),
                  pl.BlockSpec((d, tn), lambda i, j: (0, 2 * nj + j)),
                  pl.BlockSpec((SUBLANES, tn), lambda i, j: (0, j))],
        out_specs=pl.BlockSpec((tm, tn), lambda i, j: (i, j)),
        out_shape=jax.ShapeDtypeStruct((n, d), BF16),
        scratch_shapes=[pltpu.VMEM((tm, d), BF16), pltpu.VMEM((nj, SUBLANES, tn), F32)],
        compiler_params=_params(2),
    )(xf, w_in, w_in, w_in, cw)


def _post_kernel(a_ref, w_ref, x_ref, g_ref, b_ref, wr_ref, br_ref,
                 x1_ref, x1p_ref, topi_ref, gcol_ref, cnt_ref, *, alpha, e_pad):
    tm = a_ref.shape[0]
    y = alpha * x_ref[...] + _dot(a_ref[...], w_ref[...])
    x1 = _layer_norm(y, g_ref[...], b_ref[...])
    x1_ref[...] = x1
    x1p_ref[...] = _pack_halves(x1)
    xh, xl = _split2(x1)
    w2 = wr_ref[...]
    l2 = _dot(xh, w2)
    logits = l2[:, :LANES] + l2[:, LANES:] + _dot(xl, w2[:, :LANES]) + br_ref[...]
    lt = logits.T[:e_pad]
    e_i = lax.broadcasted_iota(I32, lt.shape, 0)
    cur = lt
    ids, vals = [], []
    for _ in range(TOP_K):
        m = jnp.max(cur, axis=0, keepdims=True)
        idx = jnp.min(jnp.where(cur == m, e_i, e_pad), axis=0, keepdims=True)
        cur = jnp.where(e_i == idx, -jnp.inf, cur)
        ids.append(idx)
        vals.append(m)
    ex = [jnp.exp(v - vals[0]) for v in vals]
    den = ex[0] + ex[1] + ex[2] + ex[3]
    k8 = lax.broadcasted_iota(I32, (SUBLANES, tm), 0)
    k128 = lax.broadcasted_iota(I32, (LANES, tm), 0)
    topi = jnp.zeros((SUBLANES, tm), I32)
    gmat = jnp.zeros((LANES, tm), F32)
    for kk in range(TOP_K):
        topi = jnp.where(k8 == kk, ids[kk], topi)
        gmat = jnp.where(k128 == kk, ex[kk] / den, gmat)
    topi_ref[...] = topi
    gcol_ref[...] = gmat.T
    sel = (cur == -jnp.inf).astype(F32)
    cnt_ref[0] = jnp.broadcast_to(jnp.sum(sel, axis=1, keepdims=True), (e_pad, LANES))


def _post_mixer(a, w_out, xf, ln_g, ln_b, w_router, b_router, alpha):
    n, d = xf.shape
    tm = POST_TM
    e = w_router.shape[1]
    assert TOP_K <= e <= LANES
    e_pad = -(-e // SUBLANES) * SUBLANES
    wr = jnp.zeros((d, LANES), F32).at[:, :e].set(w_router)
    wr_hi = wr.astype(BF16)
    wr_lo = (wr - wr_hi.astype(F32)).astype(BF16)
    wr2 = jnp.concatenate([wr_hi, wr_lo], axis=1)
    br = jnp.full((1, LANES), -1e30, F32).at[0, :e].set(b_router)
    full = lambda i: (0, 0)
    return pl.pallas_call(
        functools.partial(_post_kernel, alpha=alpha, e_pad=e_pad),
        grid=(n // tm,),
        in_specs=[pl.BlockSpec((tm, d), lambda i: (i, 0)),
                  pl.BlockSpec((d, d), full),
                  pl.BlockSpec((tm, d), lambda i: (i, 0)),
                  pl.BlockSpec((1, d), full),
                  pl.BlockSpec((1, d), full),
                  pl.BlockSpec((d, 2 * LANES), full),
                  pl.BlockSpec((1, LANES), full)],
        out_specs=[pl.BlockSpec((tm, d), lambda i: (i, 0)),
                   pl.BlockSpec((tm, d // 2), lambda i: (i, 0)),
                   pl.BlockSpec((SUBLANES, tm), lambda i: (0, i)),
                   pl.BlockSpec((tm, LANES), lambda i: (i, 0)),
                   pl.BlockSpec((1, e_pad, LANES), lambda i: (i, 0, 0))],
        out_shape=[jax.ShapeDtypeStruct((n, d), F32),
                   jax.ShapeDtypeStruct((n, d // 2), U32),
                   jax.ShapeDtypeStruct((SUBLANES, n), I32),
                   jax.ShapeDtypeStruct((n, LANES), F32),
                   jax.ShapeDtypeStruct((n // tm, e_pad, LANES), F32)],
        compiler_params=_params(1),
    )(a, w_out, xf, ln_g.reshape(1, d), ln_b.reshape(1, d), wr2, br)


def _pos_kernel(topi_ref, off_ref, pos_ref, *, e_pad):
    tm = topi_ref.shape[1]
    ti = topi_ref[...]
    e_i = lax.broadcasted_iota(I32, (e_pad, tm), 0)
    hits = [e_i == ti[kk:kk + 1, :] for kk in range(TOP_K)]
    sel = hits[0] | hits[1] | hits[2] | hits[3]
    s_i = lax.broadcasted_iota(I32, (tm, tm), 0)
    t_i = lax.broadcasted_iota(I32, (tm, tm), 1)
    before = (s_i < t_i).astype(BF16)
    cum = _dot(sel.astype(BF16), before) + off_ref[0]
    k8 = lax.broadcasted_iota(I32, (SUBLANES, tm), 0)
    pos = jnp.zeros((SUBLANES, tm), F32)
    for kk in range(TOP_K):
        pk = jnp.sum(jnp.where(hits[kk], cum, 0.0), axis=0, keepdims=True)
        pos = jnp.where(k8 == kk, pk, pos)
    pos_ref[...] = pos.astype(I32)


def _positions(topi, blk_off, e_pad):
    n = topi.shape[1]
    tm = POS_TM
    return pl.pallas_call(
        functools.partial(_pos_kernel, e_pad=e_pad),
        grid=(n // tm,),
        in_specs=[pl.BlockSpec((SUBLANES, tm), lambda i: (0, i)),
                  pl.BlockSpec((1, e_pad, 1), lambda i: (i, 0, 0))],
        out_specs=pl.BlockSpec((SUBLANES, tm), lambda i: (0, i)),
        out_shape=jax.ShapeDtypeStruct((SUBLANES, n), I32),
        compiler_params=_params(1),
    )(topi, blk_off)


def _row_tables(pos, sched, n_rows):
    n = pos.shape[1]
    tb = EXPERT_ROWS
    nk = n * TOP_K
    rows = jnp.arange(n_rows, dtype=I32)
    real_before = jnp.where(rows // tb < sched["n_used"][0], sched["grp_end"][sched["blk_e"][rows // tb]], nk)
    spare = nk + rows - real_before
    slots = (jnp.arange(n, dtype=I32)[None, :] * TOP_K + jnp.arange(TOP_K, dtype=I32)[:, None]).reshape(-1)
    row_slot = spare.at[pos[:TOP_K].reshape(-1)].set(slots, unique_indices=True)
    row_tok = jnp.where(row_slot < nk, row_slot // TOP_K, 0)
    return row_tok.reshape(n_rows // tb, 1, tb), row_slot.reshape(n_rows // tb, 1, tb)


def _expert_up_kernel(blk_ref, col_ref, exp_ref, flag_ref, cur_ref, nxt_ref, x_hbm, wg_ref, wu_ref, bg_ref, bu_ref,
                      act_ref, wgb_ref, wub_ref, xg0_ref, xg1_ref, sem):
    del blk_ref, col_ref, exp_ref
    t = pl.program_id(0)
    flag = flag_ref[t]
    tb = xg0_ref.shape[0]
    n_chunks = act_ref.shape[1] // MXU_COLS

    def row_copy(rows_ref, r, dst_ref, s):
        return pltpu.make_async_copy(x_hbm.at[pl.ds(rows_ref[0, 0, r], 1)], dst_ref.at[pl.ds(r, 1)], sem.at[s])

    def gather_rolled(rows_ref, dst_ref, s):
        def issue(r, carry):
            row_copy(rows_ref, r, dst_ref, s).start()
            return carry

        lax.fori_loop(0, tb, issue, 0)

    @pl.when(t == 0)
    def _():
        gather_rolled(cur_ref, xg0_ref, 0)

    @pl.when((flag & 2) != 0)
    def _():
        wgb_ref[...] = wg_ref[...].astype(BF16)
        wub_ref[...] = wu_ref[...].astype(BF16)

    def step(cur_buf, nxt_buf, s):
        pltpu.make_async_copy(x_hbm.at[pl.ds(0, tb)], cur_buf, sem.at[s]).wait()

        @pl.when((flag & 1) != 0)
        def _():
            lo, hi = _unpack_halves(cur_buf[...])
            xb = jnp.concatenate([lo.astype(BF16), hi.astype(BF16)], axis=1)
            for c in range(n_chunks):
                for r in range(c * tb // n_chunks, (c + 1) * tb // n_chunks):
                    row_copy(nxt_ref, r, nxt_buf, 1 - s).start()
                cs = slice(c * MXU_COLS, (c + 1) * MXU_COLS)
                g = jnp.minimum(_dot(xb, wgb_ref[:, cs]) + bg_ref[:, cs], SWIGLU_LIMIT)
                u = jnp.clip(_dot(xb, wub_ref[:, cs]) + bu_ref[:, cs], -SWIGLU_LIMIT, SWIGLU_LIMIT)
                act_ref[:, cs] = ((u + 1.0) * (g * jax.nn.sigmoid(SWIGLU_ALPHA * g))).astype(act_ref.dtype)

        @pl.when((flag & 1) == 0)
        def _():
            @pl.when(t + 1 < pl.num_programs(0))
            def _():
                gather_rolled(nxt_ref, nxt_buf, 1 - s)

            act_ref[...] = jnp.zeros_like(act_ref)

    @pl.when(t % 2 == 0)
    def _():
        step(xg0_ref, xg1_ref, 0)

    @pl.when(t % 2 == 1)
    def _():
        step(xg1_ref, xg0_ref, 1)


def _expert_down_kernel(be_ref, nu_ref, first_ref, prev_ref, cur_ref, act_ref, wd_ref, bd_ref, ys_hbm,
                        wdb_ref, yb0_ref, yb1_ref, sem):
    del be_ref
    i = pl.program_id(0)
    nb = pl.num_programs(0)
    tb, half = yb0_ref.shape
    n_chunks = half // MXU_COLS

    def row_copy(rows_ref, r, src_ref, s):
        return pltpu.make_async_copy(src_ref.at[pl.ds(r, 1)], ys_hbm.at[pl.ds(rows_ref[0, 0, r], 1)], sem.at[s])

    def scatter_rolled(rows_ref, src_ref, s):
        def issue(r, carry):
            row_copy(rows_ref, r, src_ref, s).start()
            return carry

        lax.fori_loop(0, tb, issue, 0)

    def wait_rows(src_ref, s):
        pltpu.make_async_copy(src_ref, ys_hbm.at[pl.ds(0, tb)], sem.at[s]).wait()

    @pl.when(i == 0)
    def _():
        yb1_ref[...] = jnp.zeros_like(yb1_ref)

    @pl.when(first_ref[i] != 0)
    def _():
        wdb_ref[...] = wd_ref[...].astype(BF16)

    def step(cur_buf, prev_buf, s):
        @pl.when(i >= 1)
        def _():
            wait_rows(cur_buf, s)

        @pl.when(i < nu_ref[0])
        def _():
            act = act_ref[...]
            for c in range(n_chunks):
                for r in range(c * tb // n_chunks, (c + 1) * tb // n_chunks):
                    row_copy(prev_ref, r, prev_buf, 1 - s).start()
                lo_c = slice(c * MXU_COLS, (c + 1) * MXU_COLS)
                hi_c = slice(half + c * MXU_COLS, half + (c + 1) * MXU_COLS)
                lo = _dot(act, wdb_ref[:, lo_c]) + bd_ref[:, lo_c]
                hi = _dot(act, wdb_ref[:, hi_c]) + bd_ref[:, hi_c]
                cur_buf[:, lo_c] = _pack_halves(jnp.concatenate([lo, hi], axis=1))

        @pl.when(i >= nu_ref[0])
        def _():
            scatter_rolled(prev_ref, prev_buf, 1 - s)
            cur_buf[...] = jnp.zeros_like(cur_buf)

        @pl.when(i == nb - 1)
        def _():
            scatter_rolled(cur_ref, cur_buf, s)
            wait_rows(prev_buf, 1 - s)
            wait_rows(cur_buf, s)

    @pl.when(i % 2 == 0)
    def _():
        step(yb0_ref, yb1_ref, 0)

    @pl.when(i % 2 == 1)
    def _():
        step(yb1_ref, yb0_ref, 1)


def _experts(layer, sched, row_tok, row_slot, x1p, w_gu, b_gu, w_dn, b_dn):
    nb, _, tb = row_tok.shape
    n_rows = nb * tb
    w = x1p.shape[1]
    _, e, d, f2 = w_gu.shape
    f = f2 // 2
    fh = EXPERT_UP_COLS
    nh = f // fh
    n_items = nh * nb
    smem_rows = functools.partial(pl.BlockSpec, (1, 1, tb), memory_space=pltpu.SMEM)
    up_spec = pltpu.PrefetchScalarGridSpec(
        num_scalar_prefetch=4,
        grid=(n_items,),
        in_specs=[smem_rows(lambda t, bl, co, ex, fl: (bl[t], 0, 0)),
                  smem_rows(lambda t, bl, co, ex, fl: (bl[jnp.minimum(t + 1, n_items - 1)], 0, 0)),
                  pl.BlockSpec(memory_space=pl.ANY),
                  pl.BlockSpec((None, None, d, fh), lambda t, bl, co, ex, fl: (layer, ex[t], 0, co[t])),
                  pl.BlockSpec((None, None, d, fh), lambda t, bl, co, ex, fl: (layer, ex[t], 0, nh + co[t])),
                  pl.BlockSpec((None, None, 1, fh), lambda t, bl, co, ex, fl: (layer, ex[t], 0, co[t])),
                  pl.BlockSpec((None, None, 1, fh), lambda t, bl, co, ex, fl: (layer, ex[t], 0, nh + co[t]))],
        out_specs=pl.BlockSpec((tb, fh), lambda t, bl, co, ex, fl: (bl[t], co[t])),
        scratch_shapes=[pltpu.VMEM((d, fh), BF16), pltpu.VMEM((d, fh), BF16),
                        pltpu.VMEM((tb, w), U32), pltpu.VMEM((tb, w), U32), pltpu.SemaphoreType.DMA((2,))],
    )
    b_gu4 = b_gu.reshape(b_gu.shape[0], e, 1, f2)
    act = pl.pallas_call(
        _expert_up_kernel,
        grid_spec=up_spec,
        out_shape=jax.ShapeDtypeStruct((n_rows, f), BF16),
        compiler_params=_params(1),
    )(sched["it_blk"], sched["it_col"], sched["it_exp"], sched["it_flag"], row_tok, row_tok, x1p, w_gu, w_gu,
      b_gu4, b_gu4)

    def blk(i, nu):
        return jnp.minimum(i, nu[0] - 1)

    down_spec = pltpu.PrefetchScalarGridSpec(
        num_scalar_prefetch=3,
        grid=(nb,),
        in_specs=[smem_rows(lambda i, be, nu, fi: ((i + nb - 1) % nb, 0, 0)),
                  smem_rows(lambda i, be, nu, fi: (i, 0, 0)),
                  pl.BlockSpec((tb, f), lambda i, be, nu, fi: (blk(i, nu), 0)),
                  pl.BlockSpec((None, None, f, d), lambda i, be, nu, fi: (layer, be[i], 0, 0)),
                  pl.BlockSpec((None, None, 1, d), lambda i, be, nu, fi: (layer, be[i], 0, 0))],
        out_specs=pl.BlockSpec(memory_space=pl.ANY),
        scratch_shapes=[pltpu.VMEM((f, d), BF16), pltpu.VMEM((tb, w), U32), pltpu.VMEM((tb, w), U32),
                        pltpu.SemaphoreType.DMA((2,))],
    )
    return pl.pallas_call(
        _expert_down_kernel,
        grid_spec=down_spec,
        out_shape=jax.ShapeDtypeStruct((n_rows, w), U32),
        compiler_params=_params(1),
    )(sched["blk_e"], sched["n_used"], sched["blk_first"], row_slot, row_slot, act, w_dn,
      b_dn.reshape(b_dn.shape[0], e, 1, d))


def _expert_schedule(total, n_rows, e, f):
    tb = EXPERT_ROWS
    nh = f // EXPERT_UP_COLS
    nb = n_rows // tb
    padded = (total + tb - 1) // tb * tb
    pad_end = jnp.cumsum(padded)
    pad_start = pad_end - padded
    n_used = pad_end[-1] // tb
    blocks = jnp.arange(nb, dtype=I32)
    blk_e = jnp.minimum(jnp.sum(pad_end[None, :] <= (blocks * tb)[:, None], axis=1), e - 1).astype(I32)
    blk_e = jnp.where(blocks < n_used, blk_e, blk_e[n_used - 1])
    blk_first = jnp.concatenate([jnp.ones((1,), I32), (blk_e[1:] != blk_e[:-1]).astype(I32)])
    items = jnp.arange(nh * nb, dtype=I32)
    grp_end = nh * (pad_end // tb)
    it_e = jnp.minimum(jnp.sum(grp_end[None, :] <= items[:, None], axis=1), e - 1).astype(I32)
    grp_blocks = jnp.maximum((padded // tb)[it_e], 1)
    r = items - nh * (pad_start // tb)[it_e]
    valid = items < nh * n_used
    spare = items - nh * n_used
    it_blk = jnp.where(valid, (pad_start // tb)[it_e] + r % grp_blocks, n_used + spare // nh)
    it_col = jnp.where(valid, r // grp_blocks, spare % nh)
    it_exp = jnp.where(valid, it_e, blk_e[n_used - 1])
    it_flag = jnp.where(valid, 1 + 2 * (r % grp_blocks == 0).astype(I32), 0)
    it_flag = it_flag.at[0].set(it_flag[0] | 2)
    return dict(pad_start=pad_start, grp_end=jnp.cumsum(total).astype(I32), blk_e=blk_e,
                n_used=n_used.astype(I32).reshape(1), blk_first=blk_first,
                it_blk=it_blk.astype(I32), it_col=it_col.astype(I32), it_exp=it_exp.astype(I32),
                it_flag=it_flag.astype(I32))


def _combine_kernel(ys_ref, x1_ref, gcol_ref, p_ref, g_ref, b_ref, wpg_ref, wpp_ref, out_ref, *, alpha):
    w = ys_ref.shape[1] // TOP_K
    gates = gcol_ref[...]
    lo = hi = None
    for kk in range(TOP_K):
        l, h = _unpack_halves(ys_ref[:, kk * w:(kk + 1) * w])
        gk = gates[:, kk:kk + 1]
        lo = gk * l if lo is None else lo + gk * l
        hi = gk * h if hi is None else hi + gk * h
    ffn = jnp.concatenate([lo, hi], axis=1)
    x2 = _layer_norm(alpha * x1_ref[...] + ffn, g_ref[...], b_ref[...])
    gate = jax.nn.sigmoid(_dot(x2.astype(BF16), wpg_ref[...]))
    out_ref[...] = x2 + gate * _dot(p_ref[...].astype(BF16), wpp_ref[...])


def _combine(ys, x1, gcol, p, ln_g, ln_b, w_pg, w_pp, alpha):
    n, d = x1.shape
    tm = COMBINE_TM
    pd = p.shape[1]
    full = lambda i: (0, 0)
    ys4 = ys.reshape(ys.shape[0] // TOP_K, TOP_K * ys.shape[1])
    return pl.pallas_call(
        functools.partial(_combine_kernel, alpha=alpha),
        grid=(n // tm,),
        in_specs=[pl.BlockSpec((tm, ys4.shape[1]), lambda i: (i, 0)),
                  pl.BlockSpec((tm, d), lambda i: (i, 0)),
                  pl.BlockSpec((tm, LANES), lambda i: (i, 0)),
                  pl.BlockSpec((tm, pd), lambda i: (i, 0)),
                  pl.BlockSpec((1, d), full),
                  pl.BlockSpec((1, d), full),
                  pl.BlockSpec((d, d), full),
                  pl.BlockSpec((pd, d), full)],
        out_specs=pl.BlockSpec((tm, d), lambda i: (i, 0)),
        out_shape=jax.ShapeDtypeStruct((n, d), F32),
        compiler_params=_params(1),
    )(ys4, x1, gcol, p, ln_g.reshape(1, d), ln_b.reshape(1, d), w_pg, w_pp)


def _moe_tail(layer, x1, x1p, topi, gcol, cnt, p, ln_g, ln_b, w_gu, b_gu, w_dn, b_dn, w_pg, w_pp, alpha):
    n, d = x1.shape
    e = w_gu.shape[1]
    e_pad = cnt.shape[1]
    tb = EXPERT_ROWS
    c = cnt[:, :, 0].astype(I32)
    c = c.reshape(n // POS_TM, POS_TM // POST_TM, e_pad).sum(axis=1)
    n_rows = -(-n * TOP_K // tb) * tb + e * tb
    sched = _expert_schedule(c.sum(axis=0), n_rows, e, w_gu.shape[3] // 2)
    blk_off = (sched["pad_start"][None, :] + jnp.cumsum(c, axis=0) - c).astype(F32)[:, :, None]

    pos = _positions(topi, blk_off, e_pad)
    row_tok, row_slot = _row_tables(pos, sched, n_rows)
    ys = _experts(layer, sched, row_tok, row_slot, x1p, w_gu, b_gu, w_dn, b_dn)
    return _combine(ys, x1, gcol, p, ln_g, ln_b, w_pg, w_pp, alpha)


def kernel(x, p, ln_g, ln_b, w_in_a, b_gate_a, norm_a, w_out_a, w_in_b, conv_b, w_out_b, w_router, b_router,
           w_gu, b_gu, w_dn, b_dn, w_ple_gate, w_ple_proj):
    bsz, seq, d = x.shape
    depth = ln_g.shape[0]
    n = bsz * seq
    alpha = (2 * depth) ** 0.25
    heads = b_gate_a.shape[-1] // 2
    dv = d // heads
    dk = dv // 2
    qkvo = 2 * heads * dk + 2 * heads * dv
    xf = x.reshape(n, d)
    for i in range(depth):
        j = i // 2
        if i % 2 == 0:
            col, row = _mlstm_gates(xf, w_in_a[j][:, qkvo:], b_gate_a[j], heads)
            proj = _proj(xf, w_in_a[j][:, :qkvo].astype(BF16), PROJ_TM, PROJ_TN_A)
            a = _mlstm(proj, col, row, norm_a[j], bsz, seq, heads, dk, dv)
            w_out = w_out_a[j]
        else:
            a = _conv_front(xf, w_in_b[j].astype(BF16), conv_b[j], seq)
            w_out = w_out_b[j]
        x1, x1p, topi, gcol, cnt = _post_mixer(a, w_out.astype(BF16), xf, ln_g[i, 0], ln_b[i, 0],
                                               w_router[i], b_router[i], alpha)
        xf = _moe_tail(i, x1, x1p, topi, gcol, cnt, p[i].reshape(n, -1), ln_g[i, 1], ln_b[i, 1],
                       w_gu, b_gu, w_dn, b_dn,
                       w_ple_gate[i].astype(BF16), w_ple_proj[i].astype(BF16), alpha)
    return xf.reshape(bsz, seq, d)
```

```python
import functools

import jax
import jax.numpy as jnp
from jax import lax
from jax.experimental import pallas as pl
from jax.experimental.pallas import tpu as pltpu

F32 = jnp.float32
BF16 = jnp.bfloat16
U32 = jnp.uint32
I32 = jnp.int32

GATE_SOFTCAP = 15.0
CONV_WIDTH = 3
TOP_K = 4
SWIGLU_ALPHA = 1.702
SWIGLU_LIMIT = 7.0
LN_EPS = 1e-5
RMS_EPS = 1e-6

LANES = 128
SUBLANES = 8
MXU_COLS = 256
VMEM_LIMIT_BYTES = 60000 * 1024

MLSTM_ROWS = 256
PROJ_TM = 1024
PROJ_TN_A = 1024
PROJ_TN_B = 512
POST_TM = 512
POS_TM = 512
DISPATCH_TM = 512
EXPERT_ROWS = 512
EXPERT_UP_COLS = 1024
COMBINE_TM = 256
DMA_UNROLL = 8


def _dot(a, b):
    return jnp.dot(a, b, preferred_element_type=F32)


def _dot_nt(a, b):
    return lax.dot_general(a, b, (((1,), (1,)), ((), ())), preferred_element_type=F32)


def _dot_tn(a, b):
    return lax.dot_general(a, b, (((0,), (0,)), ((), ())), preferred_element_type=F32)


def _split2(x):
    hi = x.astype(BF16)
    lo = (x - hi.astype(F32)).astype(BF16)
    return hi, lo


def _split3(x):
    hi = x.astype(BF16)
    r = x - hi.astype(F32)
    mid = r.astype(BF16)
    lo = (r - mid.astype(F32)).astype(BF16)
    return hi, mid, lo


def _pack_halves(y):
    half = y.shape[1] // 2
    lo = lax.bitcast_convert_type(y[:, :half].astype(BF16).astype(F32), U32) >> 16
    hi = lax.bitcast_convert_type(y[:, half:].astype(BF16).astype(F32), U32) & jnp.uint32(0xFFFF0000)
    return lo | hi


def _unpack_halves(u):
    lo = lax.bitcast_convert_type(u << 16, F32)
    hi = lax.bitcast_convert_type(u & jnp.uint32(0xFFFF0000), F32)
    return lo, hi


def _layer_norm(y, g, b):
    mu = jnp.mean(y, axis=-1, keepdims=True)
    yc = y - mu
    var = jnp.mean(yc * yc, axis=-1, keepdims=True)
    return yc * lax.rsqrt(var + LN_EPS) * g + b


def _params(n_axes):
    return pltpu.CompilerParams(dimension_semantics=("arbitrary",) * n_axes,
                                vmem_limit_bytes=VMEM_LIMIT_BYTES)


def _gate_kernel(x_ref, w_ref, b_ref, col_ref, row_ref, *, heads):
    rows = x_ref.shape[0]
    xh, xl = _split2(x_ref[...])
    wh, wl = _split2(w_ref[...])
    g = _dot(xh, wh) + _dot(xh, wl) + _dot(xl, wh) + b_ref[...]
    g = GATE_SOFTCAP * jnp.tanh(g / GATE_SOFTCAP)
    lf = -jnp.log1p(jnp.exp(-g))
    r_i = lax.broadcasted_iota(I32, (rows, rows), 0)
    c_i = lax.broadcasted_iota(I32, (rows, rows), 1)
    tril = (c_i <= r_i).astype(BF16)
    p0, p1, p2 = _split3(lf)
    bsum = _dot(tril, p0) + _dot(tril, p1) + _dot(tril, p2)
    lane = lax.broadcasted_iota(I32, g.shape, 1)
    col = jnp.where(lane < heads, g, jnp.where(lane < 2 * heads, bsum, 0.0))
    col_ref[...] = col
    row_ref[...] = col.T[: row_ref.shape[0]]


def _mlstm_gates(xf, w_g, b_g, heads):
    n, d = xf.shape
    rows = MLSTM_ROWS
    r_pad = -(-2 * heads // SUBLANES) * SUBLANES
    wg = jnp.zeros((d, LANES), F32).at[:, : 2 * heads].set(w_g)
    bg = jnp.zeros((1, LANES), F32).at[0, : 2 * heads].set(b_g)
    return pl.pallas_call(
        functools.partial(_gate_kernel, heads=heads),
        grid=(n // rows,),
        in_specs=[pl.BlockSpec((rows, d), lambda i: (i, 0)),
                  pl.BlockSpec((d, LANES), lambda i: (0, 0)),
                  pl.BlockSpec((1, LANES), lambda i: (0, 0))],
        out_specs=[pl.BlockSpec((rows, LANES), lambda i: (i, 0)),
                   pl.BlockSpec((r_pad, rows), lambda i: (0, i))],
        out_shape=[jax.ShapeDtypeStruct((n, LANES), F32),
                   jax.ShapeDtypeStruct((r_pad, n), F32)],
        compiler_params=_params(1),
    )(xf, wg, bg)


def _proj_kernel(x_ref, w_ref, o_ref, xb_ref):
    @pl.when(pl.program_id(1) == 0)
    def _():
        xb_ref[...] = x_ref[...].astype(BF16)

    o_ref[...] = _dot(xb_ref[...], w_ref[...]).astype(o_ref.dtype)


def _proj(xf, w, tm, tn):
    n, d = xf.shape
    m = w.shape[1]
    while m % tn:
        tn //= 2
    return pl.pallas_call(
        _proj_kernel,
        grid=(n // tm, m // tn),
        in_specs=[pl.BlockSpec((tm, d), lambda i, j: (i, 0)),
                  pl.BlockSpec((d, tn), lambda i, j: (0, j))],
        out_specs=pl.BlockSpec((tm, tn), lambda i, j: (i, j)),
        out_shape=jax.ShapeDtypeStruct((n, m), BF16),
        scratch_shapes=[pltpu.VMEM((tm, d), BF16)],
        compiler_params=_params(2),
    )(xf, w)


def _mlstm_kernel(q_ref, k_ref, v_ref, o_ref, col_ref, row_ref, norm_ref, out_ref, c_ref, n_ref, *, heads, dk, dv):
    rows = q_ref.shape[0]
    scale = dk ** -0.5

    @pl.when(pl.program_id(1) == 0)
    def _():
        c_ref[...] = jnp.zeros_like(c_ref)
        n_ref[...] = jnp.zeros_like(n_ref)

    col = col_ref[...]
    row = row_ref[...]
    t_i = lax.broadcasted_iota(I32, (rows, rows), 0)
    s_i = lax.broadcasted_iota(I32, (rows, rows), 1)
    causal = s_i <= t_i
    for h in range(heads):
        q = q_ref[:, h * dk:(h + 1) * dk]
        k = k_ref[:, h * dk:(h + 1) * dk]
        v = v_ref[:, h * dv:(h + 1) * dv]
        li_c = col[:, h:h + 1]
        b_c = col[:, heads + h:heads + h + 1]
        li_r = row[h:h + 1, :]
        b_r = row[heads + h:heads + h + 1, :]
        b_last = b_c[rows - 1:rows, :]
        decay_w = jnp.where(causal, jnp.exp(b_c - b_r + li_r), 0.0)
        sw = _dot_nt(q, k) * scale * decay_w
        eb = jnp.exp(b_c) * scale
        c_prev = c_ref[h]
        n_prev = n_ref[h]
        num = _dot(sw.astype(BF16), v) + eb * _dot(q, c_prev.astype(BF16))
        qn = jnp.sum(q.astype(F32) * n_prev, axis=-1, keepdims=True)
        den = jnp.sum(sw, axis=-1, keepdims=True) + eb * qn
        hh = num / jnp.maximum(jnp.abs(den), 1.0)
        wk = jnp.exp(b_last - b_c + li_c)
        carry = jnp.exp(b_last)
        kf = k.astype(F32) * wk
        c_ref[h] = carry * c_prev + _dot_tn(kf.astype(BF16), v)
        n_ref[h] = carry * n_prev + jnp.sum(kf, axis=0, keepdims=True)
        hh = hh * lax.rsqrt(jnp.mean(hh * hh, axis=-1, keepdims=True) + RMS_EPS)
        hh = hh * norm_ref[:, h * dv:(h + 1) * dv]
        og = o_ref[:, h * dv:(h + 1) * dv].astype(F32)
        out_ref[:, h * dv:(h + 1) * dv] = (hh * jax.nn.sigmoid(og)).astype(out_ref.dtype)


def _mlstm(proj, col, row, norm, bsz, seq, heads, dk, dv):
    rows = MLSTM_ROWS
    nc = seq // rows
    qk_w, v_w = heads * dk, heads * dv
    assert v_w == 2 * qk_w
    r_pad = row.shape[0]
    tok = lambda b, c: b * nc + c
    return pl.pallas_call(
        functools.partial(_mlstm_kernel, heads=heads, dk=dk, dv=dv),
        grid=(bsz, nc),
        in_specs=[pl.BlockSpec((rows, qk_w), lambda b, c: (tok(b, c), 0)),
                  pl.BlockSpec((rows, qk_w), lambda b, c: (tok(b, c), 1)),
                  pl.BlockSpec((rows, v_w), lambda b, c: (tok(b, c), 1)),
                  pl.BlockSpec((rows, v_w), lambda b, c: (tok(b, c), 2)),
                  pl.BlockSpec((rows, LANES), lambda b, c: (tok(b, c), 0)),
                  pl.BlockSpec((r_pad, rows), lambda b, c: (0, tok(b, c))),
                  pl.BlockSpec((1, v_w), lambda b, c: (0, 0))],
        out_specs=pl.BlockSpec((rows, v_w), lambda b, c: (tok(b, c), 0)),
        out_shape=jax.ShapeDtypeStruct((bsz * seq, v_w), BF16),
        scratch_shapes=[pltpu.VMEM((heads, dk, dv), F32), pltpu.VMEM((heads, 1, dk), F32)],
        compiler_params=_params(2),
    )(proj, proj, proj, proj, col, row, norm.reshape(1, v_w))


def _conv_kernel(x_ref, wb_ref, wc_ref, wu_ref, cw_ref, o_ref, xb_ref, carry_ref, *, steps_per_seq):
    i, j = pl.program_id(0), pl.program_id(1)
    tm = x_ref.shape[0]

    @pl.when(j == 0)
    def _():
        xb_ref[...] = x_ref[...].astype(BF16)

    xb = xb_ref[...]
    z = _dot(xb, wc_ref[...]) * _dot(xb, wu_ref[...])
    @pl.when(i % steps_per_seq == 0)
    def _():
        carry_ref[j] = jnp.zeros(carry_ref.shape[1:], F32)

    prev = carry_ref[j]
    carry_ref[j] = z[tm - SUBLANES:, :]
    r_i = lax.broadcasted_iota(I32, z.shape, 0)
    z1 = jnp.where(r_i == 0, prev[SUBLANES - 1:SUBLANES, :], pltpu.roll(z, 1, 0))
    z2 = jnp.where(r_i == 0, prev[SUBLANES - 2:SUBLANES - 1, :],
                   jnp.where(r_i == 1, prev[SUBLANES - 1:SUBLANES, :], pltpu.roll(z, 2, 0)))
    cw = cw_ref[...]
    zc = cw[0:1, :] * z2 + cw[1:2, :] * z1 + cw[2:3, :] * z
    o_ref[...] = (_dot(xb, wb_ref[...]) * zc).astype(o_ref.dtype)


def _conv_front(xf, w_in, conv_w, seq):
    n, d = xf.shape
    tm, tn = PROJ_TM, PROJ_TN_B
    nj = d // tn
    cw = jnp.zeros((SUBLANES, d), F32).at[:CONV_WIDTH].set(conv_w)
    return pl.pallas_call(
        functools.partial(_conv_kernel, steps_per_seq=seq // tm),
        grid=(n // tm, nj),
        in_specs=[pl.BlockSpec((tm, d), lambda i, j: (i, 0)),
                  pl.BlockSpec((d, tn), lambda i, j: (0, j)),
                  pl.BlockSpec((d, tn), lambda i, j: (0, nj + j)),
                  pl.BlockSpec((d, tn), lambda i, j: (0, 2 * nj + j)),
                  pl.BlockSpec((SUBLANES, tn), lambda i, j: (0, j))],
        out_specs=pl.BlockSpec((tm, tn), lambda i, j: (i, j)),
        out_shape=jax.ShapeDtypeStruct((n, d), BF16),
        scratch_shapes=[pltpu.VMEM((tm, d), BF16), pltpu.VMEM((nj, SUBLANES, tn), F32)],
        compiler_params=_params(2),
    )(xf, w_in, w_in, w_in, cw)


def _post_kernel(a_ref, w_ref, x_ref, g_ref, b_ref, wr_ref, br_ref,
                 x1_ref, x1p_ref, topi_ref, gcol_ref, cnt_ref, *, alpha, e_pad):
    tm = a_ref.shape[0]
    y = alpha * x_ref[...] + _dot(a_ref[...], w_ref[...])
    x1 = _layer_norm(y, g_ref[...], b_ref[...])
    x1_ref[...] = x1
    x1p_ref[...] = _pack_halves(x1)
    xh, xl = _split2(x1)
    w2 = wr_ref[...]
    l2 = _dot(xh, w2)
    logits = l2[:, :LANES] + l2[:, LANES:] + _dot(xl, w2[:, :LANES]) + br_ref[...]
    lt = logits.T[:e_pad]
    e_i = lax.broadcasted_iota(I32, lt.shape, 0)
    cur = lt
    ids, vals = [], []
    for _ in range(TOP_K):
        m = jnp.max(cur, axis=0, keepdims=True)
        idx = jnp.min(jnp.where(cur == m, e_i, e_pad), axis=0, keepdims=True)
        cur = jnp.where(e_i == idx, -jnp.inf, cur)
        ids.append(idx)
        vals.append(m)
    ex = [jnp.exp(v - vals[0]) for v in vals]
    den = ex[0] + ex[1] + ex[2] + ex[3]
    k8 = lax.broadcasted_iota(I32, (SUBLANES, tm), 0)
    k128 = lax.broadcasted_iota(I32, (LANES, tm), 0)
    topi = jnp.zeros((SUBLANES, tm), I32)
    gmat = jnp.zeros((LANES, tm), F32)
    for kk in range(TOP_K):
        topi = jnp.where(k8 == kk, ids[kk], topi)
        gmat = jnp.where(k128 == kk, ex[kk] / den, gmat)
    topi_ref[...] = topi
    gcol_ref[...] = gmat.T
    sel = (cur == -jnp.inf).astype(F32)
    cnt_ref[0] = jnp.broadcast_to(jnp.sum(sel, axis=1, keepdims=True), (e_pad, LANES))


def _post_mixer(a, w_out, xf, ln_g, ln_b, w_router, b_router, alpha):
    n, d = xf.shape
    tm = POST_TM
    e = w_router.shape[1]
    assert TOP_K <= e <= LANES
    e_pad = -(-e // SUBLANES) * SUBLANES
    wr = jnp.zeros((d, LANES), F32).at[:, :e].set(w_router)
    wr_hi = wr.astype(BF16)
    wr_lo = (wr - wr_hi.astype(F32)).astype(BF16)
    wr2 = jnp.concatenate([wr_hi, wr_lo], axis=1)
    br = jnp.full((1, LANES), -1e30, F32).at[0, :e].set(b_router)
    full = lambda i: (0, 0)
    once = pl.Buffered(1)
    return pl.pallas_call(
        functools.partial(_post_kernel, alpha=alpha, e_pad=e_pad),
        grid=(n // tm,),
        in_specs=[pl.BlockSpec((tm, d), lambda i: (i, 0)),
                  pl.BlockSpec((d, d), full, pipeline_mode=once),
                  pl.BlockSpec((tm, d), lambda i: (i, 0)),
                  pl.BlockSpec((1, d), full),
                  pl.BlockSpec((1, d), full),
                  pl.BlockSpec((d, 2 * LANES), full, pipeline_mode=once),
                  pl.BlockSpec((1, LANES), full)],
        out_specs=[pl.BlockSpec((tm, d), lambda i: (i, 0)),
                   pl.BlockSpec((tm, d // 2), lambda i: (i, 0)),
                   pl.BlockSpec((SUBLANES, tm), lambda i: (0, i)),
                   pl.BlockSpec((tm, LANES), lambda i: (i, 0)),
                   pl.BlockSpec((1, e_pad, LANES), lambda i: (i, 0, 0))],
        out_shape=[jax.ShapeDtypeStruct((n, d), F32),
                   jax.ShapeDtypeStruct((n, d // 2), U32),
                   jax.ShapeDtypeStruct((SUBLANES, n), I32),
                   jax.ShapeDtypeStruct((n, LANES), F32),
                   jax.ShapeDtypeStruct((n // tm, e_pad, LANES), F32)],
        compiler_params=_params(1),
    )(a, w_out, xf, ln_g.reshape(1, d), ln_b.reshape(1, d), wr2, br)


def _pos_kernel(topi_ref, off_ref, pos_ref, *, e_pad):
    tm = topi_ref.shape[1]
    ti = topi_ref[...]
    e_i = lax.broadcasted_iota(I32, (e_pad, tm), 0)
    hits = [e_i == ti[kk:kk + 1, :] for kk in range(TOP_K)]
    sel = hits[0] | hits[1] | hits[2] | hits[3]
    s_i = lax.broadcasted_iota(I32, (tm, tm), 0)
    t_i = lax.broadcasted_iota(I32, (tm, tm), 1)
    before = (s_i < t_i).astype(BF16)
    cum = _dot(sel.astype(BF16), before) + off_ref[0]
    k8 = lax.broadcasted_iota(I32, (SUBLANES, tm), 0)
    pos = jnp.zeros((SUBLANES, tm), F32)
    for kk in range(TOP_K):
        pk = jnp.sum(jnp.where(hits[kk], cum, 0.0), axis=0, keepdims=True)
        pos = jnp.where(k8 == kk, pk, pos)
    pos_ref[...] = pos.astype(I32)


def _positions(topi, blk_off, e_pad):
    n = topi.shape[1]
    tm = POS_TM
    return pl.pallas_call(
        functools.partial(_pos_kernel, e_pad=e_pad),
        grid=(n // tm,),
        in_specs=[pl.BlockSpec((SUBLANES, tm), lambda i: (0, i)),
                  pl.BlockSpec((1, e_pad, 1), lambda i: (i, 0, 0))],
        out_specs=pl.BlockSpec((SUBLANES, tm), lambda i: (0, i)),
        out_shape=jax.ShapeDtypeStruct((SUBLANES, n), I32),
        compiler_params=_params(1),
    )(topi, blk_off)


def _blocked_pos(pos, tm):
    n = pos.shape[1]
    p = pos[:TOP_K].reshape(TOP_K, n // tm, tm)
    return jnp.transpose(p, (1, 0, 2)).reshape(n // tm, 1, TOP_K * tm)


def _dispatch_kernel(fill_ref, pos_ref, x_ref, xs_ref, zero_ref, sem, fill_sem):
    tm = x_ref.shape[0]
    tb = zero_ref.shape[0]

    @pl.when(pl.program_id(0) == 0)
    def _():
        zero_ref[...] = jnp.zeros_like(zero_ref)

        def each_copy(b, action):
            cnt = fill_ref[b]
            first = (b + 1) * tb - cnt
            for r in range(SUBLANES - 1):
                @pl.when(r < cnt % SUBLANES)
                def _(r=r):
                    action(pltpu.make_async_copy(zero_ref.at[pl.ds(0, 1)], xs_ref.at[pl.ds(first + r, 1)], fill_sem))

            groups = cnt // SUBLANES
            row = (b + 1) * tb - groups * SUBLANES
            bit = tb // SUBLANES
            while bit:
                @pl.when((groups & bit) != 0)
                def _(row=row, bit=bit):
                    rows = bit * SUBLANES
                    dst = xs_ref.at[pl.ds(pl.multiple_of(row, SUBLANES), rows)]
                    action(pltpu.make_async_copy(zero_ref.at[pl.ds(0, rows)], dst, fill_sem))

                row = row + (groups & bit) * SUBLANES
                bit //= 2

        def start_all(b, carry):
            each_copy(b, lambda cp: cp.start())
            return carry

        def wait_all(b, carry):
            each_copy(b, lambda cp: cp.wait())
            return carry

        lax.fori_loop(0, fill_ref.shape[0], start_all, 0)
        lax.fori_loop(0, fill_ref.shape[0], wait_all, 0)

    def issue(g, carry):
        for u in range(DMA_UNROLL):
            t = pl.multiple_of(g * DMA_UNROLL, DMA_UNROLL) + u
            for kk in range(TOP_K):
                p = pos_ref[0, 0, kk * tm + t]
                pltpu.make_async_copy(x_ref.at[pl.ds(t, 1)], xs_ref.at[pl.ds(p, 1)], sem).start()
        return carry

    lax.fori_loop(0, tm // DMA_UNROLL, issue, 0)
    for _ in range(TOP_K):
        pltpu.make_async_copy(x_ref, xs_ref.at[pl.ds(0, tm)], sem).wait()


def _dispatch(blk_fill, pos_blocked, x1p, n_rows):
    n, w = x1p.shape
    tm = DISPATCH_TM
    grid_spec = pltpu.PrefetchScalarGridSpec(
        num_scalar_prefetch=1,
        grid=(n // tm,),
        in_specs=[pl.BlockSpec((1, 1, TOP_K * tm), lambda i, fl: (i, 0, 0), memory_space=pltpu.SMEM),
                  pl.BlockSpec((tm, w), lambda i, fl: (i, 0))],
        out_specs=pl.BlockSpec(memory_space=pl.ANY),
        scratch_shapes=[pltpu.VMEM((EXPERT_ROWS, w), U32), pltpu.SemaphoreType.DMA(()), pltpu.SemaphoreType.DMA(())],
    )
    return pl.pallas_call(
        _dispatch_kernel,
        grid_spec=grid_spec,
        out_shape=jax.ShapeDtypeStruct((n_rows, w), U32),
        compiler_params=_params(1),
    )(blk_fill, pos_blocked, x1p)


def _expert_up_kernel(blk_ref, col_ref, exp_ref, flag_ref, xs_ref, wg_ref, wu_ref, bg_ref, bu_ref, act_ref,
                      wgb_ref, wub_ref):
    del blk_ref, col_ref, exp_ref
    flag = flag_ref[pl.program_id(0)]

    @pl.when((flag & 2) != 0)
    def _():
        wgb_ref[...] = wg_ref[...].astype(BF16)
        wub_ref[...] = wu_ref[...].astype(BF16)

    @pl.when((flag & 1) != 0)
    def _():
        lo, hi = _unpack_halves(xs_ref[...])
        xb = jnp.concatenate([lo.astype(BF16), hi.astype(BF16)], axis=1)
        for c in range(0, act_ref.shape[1], MXU_COLS):
            cs = slice(c, c + MXU_COLS)
            g = jnp.minimum(_dot(xb, wgb_ref[:, cs]) + bg_ref[:, cs], SWIGLU_LIMIT)
            u = jnp.clip(_dot(xb, wub_ref[:, cs]) + bu_ref[:, cs], -SWIGLU_LIMIT, SWIGLU_LIMIT)
            act_ref[:, cs] = ((u + 1.0) * (g * jax.nn.sigmoid(SWIGLU_ALPHA * g))).astype(act_ref.dtype)

    @pl.when((flag & 1) == 0)
    def _():
        act_ref[...] = jnp.zeros_like(act_ref)


def _expert_down_kernel(be_ref, nu_ref, first_ref, act_ref, wd_ref, bd_ref, ys_ref, wdb_ref):
    del be_ref
    i = pl.program_id(0)

    @pl.when(first_ref[i] != 0)
    def _():
        wdb_ref[...] = wd_ref[...].astype(BF16)

    @pl.when(i < nu_ref[0])
    def _():
        act = act_ref[...]
        half = ys_ref.shape[1]
        for c in range(0, half, MXU_COLS):
            lo = _dot(act, wdb_ref[:, c:c + MXU_COLS]) + bd_ref[:, c:c + MXU_COLS]
            hi = _dot(act, wdb_ref[:, half + c:half + c + MXU_COLS]) + bd_ref[:, half + c:half + c + MXU_COLS]
            ys_ref[:, c:c + MXU_COLS] = _pack_halves(jnp.concatenate([lo, hi], axis=1))

    @pl.when(i >= nu_ref[0])
    def _():
        ys_ref[...] = jnp.zeros_like(ys_ref)


def _experts(layer, sched, xs, w_gu, b_gu, w_dn, b_dn):
    n_rows, w = xs.shape
    _, e, d, f2 = w_gu.shape
    f = f2 // 2
    tb, fh = EXPERT_ROWS, EXPERT_UP_COLS
    nh = f // fh
    nb = n_rows // tb
    up_spec = pltpu.PrefetchScalarGridSpec(
        num_scalar_prefetch=4,
        grid=(nh * nb,),
        in_specs=[pl.BlockSpec((tb, w), lambda t, bl, co, ex, fl: (bl[t], 0)),
                  pl.BlockSpec((None, None, d, fh), lambda t, bl, co, ex, fl: (layer, ex[t], 0, co[t])),
                  pl.BlockSpec((None, None, d, fh), lambda t, bl, co, ex, fl: (layer, ex[t], 0, nh + co[t])),
                  pl.BlockSpec((None, None, 1, fh), lambda t, bl, co, ex, fl: (layer, ex[t], 0, co[t])),
                  pl.BlockSpec((None, None, 1, fh), lambda t, bl, co, ex, fl: (layer, ex[t], 0, nh + co[t]))],
        out_specs=pl.BlockSpec((tb, fh), lambda t, bl, co, ex, fl: (bl[t], co[t])),
        scratch_shapes=[pltpu.VMEM((d, fh), BF16), pltpu.VMEM((d, fh), BF16)],
    )
    b_gu4 = b_gu.reshape(b_gu.shape[0], e, 1, f2)
    act = pl.pallas_call(
        _expert_up_kernel,
        grid_spec=up_spec,
        out_shape=jax.ShapeDtypeStruct((n_rows, f), BF16),
        compiler_params=_params(1),
    )(sched["it_blk"], sched["it_col"], sched["it_exp"], sched["it_flag"], xs, w_gu, w_gu, b_gu4, b_gu4)

    def blk(i, nu):
        return jnp.minimum(i, nu[0] - 1)

    down_spec = pltpu.PrefetchScalarGridSpec(
        num_scalar_prefetch=3,
        grid=(nb,),
        in_specs=[pl.BlockSpec((tb, f), lambda i, be, nu, fi: (blk(i, nu), 0)),
                  pl.BlockSpec((None, None, f, d), lambda i, be, nu, fi: (layer, be[i], 0, 0)),
                  pl.BlockSpec((None, None, 1, d), lambda i, be, nu, fi: (layer, be[i], 0, 0))],
        out_specs=pl.BlockSpec((tb, w), lambda i, be, nu, fi: (i, 0)),
        scratch_shapes=[pltpu.VMEM((f, d), BF16)],
    )
    return pl.pallas_call(
        _expert_down_kernel,
        grid_spec=down_spec,
        out_shape=jax.ShapeDtypeStruct((n_rows, w), U32),
        compiler_params=_params(1),
    )(sched["blk_e"], sched["n_used"], sched["blk_first"], act, w_dn, b_dn.reshape(b_dn.shape[0], e, 1, d))


def _expert_schedule(total, n_rows, e, f):
    tb = EXPERT_ROWS
    nh = f // EXPERT_UP_COLS
    nb = n_rows // tb
    padded = (total + tb - 1) // tb * tb
    pad_end = jnp.cumsum(padded)
    pad_start = pad_end - padded
    n_used = pad_end[-1] // tb
    blocks = jnp.arange(nb, dtype=I32)
    blk_e = jnp.minimum(jnp.sum(pad_end[None, :] <= (blocks * tb)[:, None], axis=1), e - 1).astype(I32)
    blk_e = jnp.where(blocks < n_used, blk_e, blk_e[n_used - 1])
    blk_first = jnp.concatenate([jnp.ones((1,), I32), (blk_e[1:] != blk_e[:-1]).astype(I32)])
    grp_last = blocks == pad_end[blk_e] // tb - 1
    blk_fill = jnp.where(blocks < n_used, jnp.where(grp_last, (padded - total)[blk_e], 0), tb).astype(I32)
    items = jnp.arange(nh * nb, dtype=I32)
    grp_end = nh * (pad_end // tb)
    it_e = jnp.minimum(jnp.sum(grp_end[None, :] <= items[:, None], axis=1), e - 1).astype(I32)
    grp_blocks = jnp.maximum((padded // tb)[it_e], 1)
    r = items - nh * (pad_start // tb)[it_e]
    valid = items < nh * n_used
    spare = items - nh * n_used
    it_blk = jnp.where(valid, (pad_start // tb)[it_e] + r % grp_blocks, n_used + spare // nh)
    it_col = jnp.where(valid, r // grp_blocks, spare % nh)
    it_exp = jnp.where(valid, it_e, blk_e[n_used - 1])
    it_flag = jnp.where(valid, 1 + 2 * (r % grp_blocks == 0).astype(I32), 0)
    it_flag = it_flag.at[0].set(it_flag[0] | 2)
    return dict(pad_start=pad_start, blk_fill=blk_fill, blk_e=blk_e, n_used=n_used.astype(I32).reshape(1),
                blk_first=blk_first, it_blk=it_blk.astype(I32), it_col=it_col.astype(I32),
                it_exp=it_exp.astype(I32), it_flag=it_flag.astype(I32))


def _combine_kernel(pos_ref, nxt_ref, ys_ref, x1_ref, gcol_ref, p_ref, g_ref, b_ref, wpg_ref, wpp_ref, out_ref,
                    buf_ref, sem, *, alpha):
    tm = x1_ref.shape[0]
    i = pl.program_id(0)
    slot = i % 2

    def gather(rows_ref, dst):
        def issue(g, carry):
            for u in range(DMA_UNROLL):
                t = pl.multiple_of(g * DMA_UNROLL, DMA_UNROLL) + u
                for kk in range(TOP_K):
                    r = rows_ref[0, 0, kk * tm + t]
                    pltpu.make_async_copy(ys_ref.at[pl.ds(r, 1)], buf_ref.at[dst, kk, pl.ds(t, 1)],
                                          sem.at[dst]).start()
            return carry

        lax.fori_loop(0, tm // DMA_UNROLL, issue, 0)

    @pl.when(i == 0)
    def _():
        gather(pos_ref, 0)

    @pl.when(i + 1 < pl.num_programs(0))
    def _():
        gather(nxt_ref, 1 - slot)

    for kk in range(TOP_K):
        pltpu.make_async_copy(ys_ref.at[pl.ds(0, tm)], buf_ref.at[slot, kk], sem.at[slot]).wait()

    gates = gcol_ref[...]
    lo = hi = None
    for kk in range(TOP_K):
        l, h = _unpack_halves(buf_ref[slot, kk])
        gk = gates[:, kk:kk + 1]
        lo = gk * l if lo is None else lo + gk * l
        hi = gk * h if hi is None else hi + gk * h
    ffn = jnp.concatenate([lo, hi], axis=1)
    x2 = _layer_norm(alpha * x1_ref[...] + ffn, g_ref[...], b_ref[...])
    gate = jax.nn.sigmoid(_dot(x2.astype(BF16), wpg_ref[...]))
    out_ref[...] = x2 + gate * _dot(p_ref[...].astype(BF16), wpp_ref[...])


def _combine(pos_blocked, ys, x1, gcol, p, ln_g, ln_b, w_pg, w_pp, alpha):
    n, d = x1.shape
    tm = COMBINE_TM
    pd = p.shape[1]
    full = lambda i: (0, 0)
    once = pl.Buffered(1)
    last = n // tm - 1
    return pl.pallas_call(
        functools.partial(_combine_kernel, alpha=alpha),
        grid=(n // tm,),
        in_specs=[pl.BlockSpec((1, 1, TOP_K * tm), lambda i: (i, 0, 0), memory_space=pltpu.SMEM),
                  pl.BlockSpec((1, 1, TOP_K * tm), lambda i: (jnp.minimum(i + 1, last), 0, 0),
                               memory_space=pltpu.SMEM),
                  pl.BlockSpec(memory_space=pl.ANY),
                  pl.BlockSpec((tm, d), lambda i: (i, 0)),
                  pl.BlockSpec((tm, LANES), lambda i: (i, 0)),
                  pl.BlockSpec((tm, pd), lambda i: (i, 0)),
                  pl.BlockSpec((1, d), full),
                  pl.BlockSpec((1, d), full),
                  pl.BlockSpec((d, d), full, pipeline_mode=once),
                  pl.BlockSpec((pd, d), full, pipeline_mode=once)],
        out_specs=pl.BlockSpec((tm, d), lambda i: (i, 0)),
        out_shape=jax.ShapeDtypeStruct((n, d), F32),
        scratch_shapes=[pltpu.VMEM((2, TOP_K, tm, d // 2), U32), pltpu.SemaphoreType.DMA((2,))],
        compiler_params=_params(1),
    )(pos_blocked, pos_blocked, ys, x1, gcol, p, ln_g.reshape(1, d), ln_b.reshape(1, d), w_pg, w_pp)


def _moe_tail(layer, x1, x1p, topi, gcol, cnt, p, ln_g, ln_b, w_gu, b_gu, w_dn, b_dn, w_pg, w_pp, alpha):
    n, d = x1.shape
    e = w_gu.shape[1]
    e_pad = cnt.shape[1]
    tb = EXPERT_ROWS
    c = cnt[:, :, 0].astype(I32)
    c = c.reshape(n // POS_TM, POS_TM // POST_TM, e_pad).sum(axis=1)
    n_rows = -(-n * TOP_K // tb) * tb + e * tb
    sched = _expert_schedule(c.sum(axis=0), n_rows, e, w_gu.shape[3] // 2)
    blk_off = (sched["pad_start"][None, :] + jnp.cumsum(c, axis=0) - c).astype(F32)[:, :, None]

    pos = _positions(topi, blk_off, e_pad)
    xs = _dispatch(sched["blk_fill"], _blocked_pos(pos, DISPATCH_TM), x1p, n_rows)
    ys = _experts(layer, sched, xs, w_gu, b_gu, w_dn, b_dn)
    return _combine(_blocked_pos(pos, COMBINE_TM), ys, x1, gcol, p, ln_g, ln_b, w_pg, w_pp, alpha)


def kernel(x, p, ln_g, ln_b, w_in_a, b_gate_a, norm_a, w_out_a, w_in_b, conv_b, w_out_b, w_router, b_router,
           w_gu, b_gu, w_dn, b_dn, w_ple_gate, w_ple_proj):
    bsz, seq, d = x.shape
    depth = ln_g.shape[0]
    n = bsz * seq
    alpha = (2 * depth) ** 0.25
    heads = b_gate_a.shape[-1] // 2
    dv = d // heads
    dk = dv // 2
    qkvo = 2 * heads * dk + 2 * heads * dv
    xf = x.reshape(n, d)
    for i in range(depth):
        j = i // 2
        if i % 2 == 0:
            col, row = _mlstm_gates(xf, w_in_a[j][:, qkvo:], b_gate_a[j], heads)
            proj = _proj(xf, w_in_a[j][:, :qkvo].astype(BF16), PROJ_TM, PROJ_TN_A)
            a = _mlstm(proj, col, row, norm_a[j], bsz, seq, heads, dk, dv)
            w_out = w_out_a[j]
        else:
            a = _conv_front(xf, w_in_b[j].astype(BF16), conv_b[j], seq)
            w_out = w_out_b[j]
        x1, x1p, topi, gcol, cnt = _post_mixer(a, w_out.astype(BF16), xf, ln_g[i, 0], ln_b[i, 0],
                                               w_router[i], b_router[i], alpha)
        xf = _moe_tail(i, x1, x1p, topi, gcol, cnt, p[i].reshape(n, -1), ln_g[i, 1], ln_b[i, 1],
                       w_gu, b_gu, w_dn, b_dn,
                       w_ple_gate[i].astype(BF16), w_ple_proj[i].astype(BF16), alpha)
    return xf.reshape(bsz, seq, d)
```

```python
import functools

import jax
import jax.numpy as jnp
from jax import lax
from jax.experimental import pallas as pl
from jax.experimental.pallas import tpu as pltpu

F32 = jnp.float32
BF16 = jnp.bfloat16
U32 = jnp.uint32
I32 = jnp.int32

GATE_SOFTCAP = 15.0
CONV_WIDTH = 3
TOP_K = 4
SWIGLU_ALPHA = 1.702
SWIGLU_LIMIT = 7.0
LN_EPS = 1e-5
RMS_EPS = 1e-6

LANES = 128
SUBLANES = 8
MXU_COLS = 256
VMEM_LIMIT_BYTES = 60000 * 1024

MLSTM_ROWS = 256
PROJ_TM = 1024
PROJ_TN_A = 1024
PROJ_TN_B = 512
POST_TM = 512
POS_TM = 512
DISPATCH_TM = 512
EXPERT_ROWS = 512
EXPERT_UP_COLS = 1024
COMBINE_TM = 256
DMA_UNROLL = 8


def _dot(a, b):
    return jnp.dot(a, b, preferred_element_type=F32)


def _dot_nt(a, b):
    return lax.dot_general(a, b, (((1,), (1,)), ((), ())), preferred_element_type=F32)


def _dot_tn(a, b):
    return lax.dot_general(a, b, (((0,), (0,)), ((), ())), preferred_element_type=F32)


def _split2(x):
    hi = x.astype(BF16)
    lo = (x - hi.astype(F32)).astype(BF16)
    return hi, lo


def _split3(x):
    hi = x.astype(BF16)
    r = x - hi.astype(F32)
    mid = r.astype(BF16)
    lo = (r - mid.astype(F32)).astype(BF16)
    return hi, mid, lo


def _pack_halves(y):
    half = y.shape[1] // 2
    lo = lax.bitcast_convert_type(y[:, :half].astype(BF16).astype(F32), U32) >> 16
    hi = lax.bitcast_convert_type(y[:, half:].astype(BF16).astype(F32), U32) & jnp.uint32(0xFFFF0000)
    return lo | hi


def _unpack_halves(u):
    lo = lax.bitcast_convert_type(u << 16, F32)
    hi = lax.bitcast_convert_type(u & jnp.uint32(0xFFFF0000), F32)
    return lo, hi


def _store_row_tiles(ref, words, first=0):
    r = words.shape[0]
    c = ref.shape[0] // r
    for j in range(words.shape[1] // LANES):
        ref[pl.ds(first + j, r, stride=c), :] = words[:, j * LANES:(j + 1) * LANES]


def _load_row_tiles(ref, r, index=()):
    c = ref.shape[-2] // r
    return [ref[index + (pl.ds(j, r, stride=c), slice(None))] for j in range(c)]


def _layer_norm(y, g, b):
    mu = jnp.mean(y, axis=-1, keepdims=True)
    yc = y - mu
    var = jnp.mean(yc * yc, axis=-1, keepdims=True)
    return yc * lax.rsqrt(var + LN_EPS) * g + b


def _params(n_axes):
    return pltpu.CompilerParams(dimension_semantics=("arbitrary",) * n_axes,
                                vmem_limit_bytes=VMEM_LIMIT_BYTES)


def _gate_kernel(x_ref, w_ref, b_ref, col_ref, row_ref, *, heads):
    rows = x_ref.shape[0]
    xh, xl = _split2(x_ref[...])
    wh, wl = _split2(w_ref[...])
    g = _dot(xh, wh) + _dot(xh, wl) + _dot(xl, wh) + b_ref[...]
    g = GATE_SOFTCAP * jnp.tanh(g / GATE_SOFTCAP)
    lf = -jnp.log1p(jnp.exp(-g))
    r_i = lax.broadcasted_iota(I32, (rows, rows), 0)
    c_i = lax.broadcasted_iota(I32, (rows, rows), 1)
    tril = (c_i <= r_i).astype(BF16)
    p0, p1, p2 = _split3(lf)
    bsum = _dot(tril, p0) + _dot(tril, p1) + _dot(tril, p2)
    lane = lax.broadcasted_iota(I32, g.shape, 1)
    col = jnp.where(lane < heads, g, jnp.where(lane < 2 * heads, bsum, 0.0))
    col_ref[...] = col
    row_ref[...] = col.T[: row_ref.shape[0]]


def _mlstm_gates(xf, w_g, b_g, heads):
    n, d = xf.shape
    rows = MLSTM_ROWS
    r_pad = -(-2 * heads // SUBLANES) * SUBLANES
    wg = jnp.zeros((d, LANES), F32).at[:, : 2 * heads].set(w_g)
    bg = jnp.zeros((1, LANES), F32).at[0, : 2 * heads].set(b_g)
    return pl.pallas_call(
        functools.partial(_gate_kernel, heads=heads),
        grid=(n // rows,),
        in_specs=[pl.BlockSpec((rows, d), lambda i: (i, 0)),
                  pl.BlockSpec((d, LANES), lambda i: (0, 0)),
                  pl.BlockSpec((1, LANES), lambda i: (0, 0))],
        out_specs=[pl.BlockSpec((rows, LANES), lambda i: (i, 0)),
                   pl.BlockSpec((r_pad, rows), lambda i: (0, i))],
        out_shape=[jax.ShapeDtypeStruct((n, LANES), F32),
                   jax.ShapeDtypeStruct((r_pad, n), F32)],
        compiler_params=_params(1),
    )(xf, wg, bg)


def _proj_kernel(x_ref, w_ref, o_ref, xb_ref):
    @pl.when(pl.program_id(1) == 0)
    def _():
        xb_ref[...] = x_ref[...].astype(BF16)

    o_ref[...] = _dot(xb_ref[...], w_ref[...]).astype(o_ref.dtype)


def _proj(xf, w, tm, tn):
    n, d = xf.shape
    m = w.shape[1]
    while m % tn:
        tn //= 2
    return pl.pallas_call(
        _proj_kernel,
        grid=(n // tm, m // tn),
        in_specs=[pl.BlockSpec((tm, d), lambda i, j: (i, 0)),
                  pl.BlockSpec((d, tn), lambda i, j: (0, j))],
        out_specs=pl.BlockSpec((tm, tn), lambda i, j: (i, j)),
        out_shape=jax.ShapeDtypeStruct((n, m), BF16),
        scratch_shapes=[pltpu.VMEM((tm, d), BF16)],
        compiler_params=_params(2),
    )(xf, w)


def _mlstm_kernel(q_ref, k_ref, v_ref, o_ref, col_ref, row_ref, norm_ref, out_ref, c_ref, n_ref, *, heads, dk, dv):
    rows = q_ref.shape[0]
    scale = dk ** -0.5

    @pl.when(pl.program_id(1) == 0)
    def _():
        c_ref[...] = jnp.zeros_like(c_ref)
        n_ref[...] = jnp.zeros_like(n_ref)

    col = col_ref[...]
    row = row_ref[...]
    t_i = lax.broadcasted_iota(I32, (rows, rows), 0)
    s_i = lax.broadcasted_iota(I32, (rows, rows), 1)
    causal = s_i <= t_i
    for h in range(heads):
        q = q_ref[:, h * dk:(h + 1) * dk]
        k = k_ref[:, h * dk:(h + 1) * dk]
        v = v_ref[:, h * dv:(h + 1) * dv]
        li_c = col[:, h:h + 1]
        b_c = col[:, heads + h:heads + h + 1]
        li_r = row[h:h + 1, :]
        b_r = row[heads + h:heads + h + 1, :]
        b_last = b_c[rows - 1:rows, :]
        decay_w = jnp.where(causal, jnp.exp(b_c - b_r + li_r), 0.0)
        sw = _dot_nt(q, k) * scale * decay_w
        eb = jnp.exp(b_c) * scale
        c_prev = c_ref[h]
        n_prev = n_ref[h]
        num = _dot(sw.astype(BF16), v) + eb * _dot(q, c_prev.astype(BF16))
        qn = jnp.sum(q.astype(F32) * n_prev, axis=-1, keepdims=True)
        den = jnp.sum(sw, axis=-1, keepdims=True) + eb * qn
        hh = num / jnp.maximum(jnp.abs(den), 1.0)
        wk = jnp.exp(b_last - b_c + li_c)
        carry = jnp.exp(b_last)
        kf = k.astype(F32) * wk
        c_ref[h] = carry * c_prev + _dot_tn(kf.astype(BF16), v)
        n_ref[h] = carry * n_prev + jnp.sum(kf, axis=0, keepdims=True)
        hh = hh * lax.rsqrt(jnp.mean(hh * hh, axis=-1, keepdims=True) + RMS_EPS)
        hh = hh * norm_ref[:, h * dv:(h + 1) * dv]
        og = o_ref[:, h * dv:(h + 1) * dv].astype(F32)
        out_ref[:, h * dv:(h + 1) * dv] = (hh * jax.nn.sigmoid(og)).astype(out_ref.dtype)


def _mlstm(proj, col, row, norm, bsz, seq, heads, dk, dv):
    rows = MLSTM_ROWS
    nc = seq // rows
    qk_w, v_w = heads * dk, heads * dv
    assert v_w == 2 * qk_w
    r_pad = row.shape[0]
    tok = lambda b, c: b * nc + c
    return pl.pallas_call(
        functools.partial(_mlstm_kernel, heads=heads, dk=dk, dv=dv),
        grid=(bsz, nc),
        in_specs=[pl.BlockSpec((rows, qk_w), lambda b, c: (tok(b, c), 0)),
                  pl.BlockSpec((rows, qk_w), lambda b, c: (tok(b, c), 1)),
                  pl.BlockSpec((rows, v_w), lambda b, c: (tok(b, c), 1)),
                  pl.BlockSpec((rows, v_w), lambda b, c: (tok(b, c), 2)),
                  pl.BlockSpec((rows, LANES), lambda b, c: (tok(b, c), 0)),
                  pl.BlockSpec((r_pad, rows), lambda b, c: (0, tok(b, c))),
                  pl.BlockSpec((1, v_w), lambda b, c: (0, 0))],
        out_specs=pl.BlockSpec((rows, v_w), lambda b, c: (tok(b, c), 0)),
        out_shape=jax.ShapeDtypeStruct((bsz * seq, v_w), BF16),
        scratch_shapes=[pltpu.VMEM((heads, dk, dv), F32), pltpu.VMEM((heads, 1, dk), F32)],
        compiler_params=_params(2),
    )(proj, proj, proj, proj, col, row, norm.reshape(1, v_w))


def _conv_kernel(x_ref, wb_ref, wc_ref, wu_ref, cw_ref, o_ref, xb_ref, carry_ref, *, steps_per_seq):
    i, j = pl.program_id(0), pl.program_id(1)
    tm = x_ref.shape[0]

    @pl.when(j == 0)
    def _():
        xb_ref[...] = x_ref[...].astype(BF16)

    xb = xb_ref[...]
    z = _dot(xb, wc_ref[...]) * _dot(xb, wu_ref[...])
    @pl.when(i % steps_per_seq == 0)
    def _():
        carry_ref[j] = jnp.zeros(carry_ref.shape[1:], F32)

    prev = carry_ref[j]
    carry_ref[j] = z[tm - SUBLANES:, :]
    r_i = lax.broadcasted_iota(I32, z.shape, 0)
    z1 = jnp.where(r_i == 0, prev[SUBLANES - 1:SUBLANES, :], pltpu.roll(z, 1, 0))
    z2 = jnp.where(r_i == 0, prev[SUBLANES - 2:SUBLANES - 1, :],
                   jnp.where(r_i == 1, prev[SUBLANES - 1:SUBLANES, :], pltpu.roll(z, 2, 0)))
    cw = cw_ref[...]
    zc = cw[0:1, :] * z2 + cw[1:2, :] * z1 + cw[2:3, :] * z
    o_ref[...] = (_dot(xb, wb_ref[...]) * zc).astype(o_ref.dtype)


def _conv_front(xf, w_in, conv_w, seq):
    n, d = xf.shape
    tm, tn = PROJ_TM, PROJ_TN_B
    nj = d // tn
    cw = jnp.zeros((SUBLANES, d), F32).at[:CONV_WIDTH].set(conv_w)
    return pl.pallas_call(
        functools.partial(_conv_kernel, steps_per_seq=seq // tm),
        grid=(n // tm, nj),
        in_specs=[pl.BlockSpec((tm, d), lambda i, j: (i, 0)),
                  pl.BlockSpec((d, tn), lambda i, j: (0, j)),
                  pl.BlockSpec((d, tn), lambda i, j: (0, nj + j)),
                  pl.BlockSpec((d, tn), lambda i, j: (0, 2 * nj + j)),
                  pl.BlockSpec((SUBLANES, tn), lambda i, j: (0, j))],
        out_specs=pl.BlockSpec((tm, tn), lambda i, j: (i, j)),
        out_shape=jax.ShapeDtypeStruct((n, d), BF16),
        scratch_shapes=[pltpu.VMEM((tm, d), BF16), pltpu.VMEM((nj, SUBLANES, tn), F32)],
        compiler_params=_params(2),
    )(xf, w_in, w_in, w_in, cw)


def _post_kernel(a_ref, w_ref, x_ref, g_ref, b_ref, wr_ref, br_ref,
                 x1_ref, x1p_ref, topi_ref, gcol_ref, cnt_ref, *, alpha, e_pad):
    tm = a_ref.shape[0]
    y = alpha * x_ref[...] + _dot(a_ref[...], w_ref[...])
    x1 = _layer_norm(y, g_ref[...], b_ref[...])
    x1_ref[...] = x1
    _store_row_tiles(x1p_ref, _pack_halves(x1))
    xh, xl = _split2(x1)
    w2 = wr_ref[...]
    l2 = _dot(xh, w2)
    logits = l2[:, :LANES] + l2[:, LANES:] + _dot(xl, w2[:, :LANES]) + br_ref[...]
    lt = logits.T[:e_pad]
    e_i = lax.broadcasted_iota(I32, lt.shape, 0)
    cur = lt
    ids, vals = [], []
    for _ in range(TOP_K):
        m = jnp.max(cur, axis=0, keepdims=True)
        idx = jnp.min(jnp.where(cur == m, e_i, e_pad), axis=0, keepdims=True)
        cur = jnp.where(e_i == idx, -jnp.inf, cur)
        ids.append(idx)
        vals.append(m)
    ex = [jnp.exp(v - vals[0]) for v in vals]
    den = ex[0] + ex[1] + ex[2] + ex[3]
    k8 = lax.broadcasted_iota(I32, (SUBLANES, tm), 0)
    k128 = lax.broadcasted_iota(I32, (LANES, tm), 0)
    topi = jnp.zeros((SUBLANES, tm), I32)
    gmat = jnp.zeros((LANES, tm), F32)
    for kk in range(TOP_K):
        topi = jnp.where(k8 == kk, ids[kk], topi)
        gmat = jnp.where(k128 == kk, ex[kk] / den, gmat)
    topi_ref[...] = topi
    gcol_ref[...] = gmat.T
    sel = (cur == -jnp.inf).astype(F32)
    cnt_ref[0] = jnp.broadcast_to(jnp.sum(sel, axis=1, keepdims=True), (e_pad, LANES))


def _post_mixer(a, w_out, xf, ln_g, ln_b, w_router, b_router, alpha):
    n, d = xf.shape
    tm = POST_TM
    e = w_router.shape[1]
    assert TOP_K <= e <= LANES
    e_pad = -(-e // SUBLANES) * SUBLANES
    wr = jnp.zeros((d, LANES), F32).at[:, :e].set(w_router)
    wr_hi = wr.astype(BF16)
    wr_lo = (wr - wr_hi.astype(F32)).astype(BF16)
    wr2 = jnp.concatenate([wr_hi, wr_lo], axis=1)
    br = jnp.full((1, LANES), -1e30, F32).at[0, :e].set(b_router)
    full = lambda i: (0, 0)
    once = pl.Buffered(1)
    chunks = d // 2 // LANES
    return pl.pallas_call(
        functools.partial(_post_kernel, alpha=alpha, e_pad=e_pad),
        grid=(n // tm,),
        in_specs=[pl.BlockSpec((tm, d), lambda i: (i, 0)),
                  pl.BlockSpec((d, d), full, pipeline_mode=once),
                  pl.BlockSpec((tm, d), lambda i: (i, 0)),
                  pl.BlockSpec((1, d), full),
                  pl.BlockSpec((1, d), full),
                  pl.BlockSpec((d, 2 * LANES), full, pipeline_mode=once),
                  pl.BlockSpec((1, LANES), full)],
        out_specs=[pl.BlockSpec((tm, d), lambda i: (i, 0)),
                   pl.BlockSpec((tm * chunks, LANES), lambda i: (i, 0)),
                   pl.BlockSpec((SUBLANES, tm), lambda i: (0, i)),
                   pl.BlockSpec((tm, LANES), lambda i: (i, 0)),
                   pl.BlockSpec((1, e_pad, LANES), lambda i: (i, 0, 0))],
        out_shape=[jax.ShapeDtypeStruct((n, d), F32),
                   jax.ShapeDtypeStruct((n * chunks, LANES), U32),
                   jax.ShapeDtypeStruct((SUBLANES, n), I32),
                   jax.ShapeDtypeStruct((n, LANES), F32),
                   jax.ShapeDtypeStruct((n // tm, e_pad, LANES), F32)],
        compiler_params=_params(1),
    )(a, w_out, xf, ln_g.reshape(1, d), ln_b.reshape(1, d), wr2, br)


def _pos_kernel(topi_ref, off_ref, pos_ref, *, e_pad):
    tm = topi_ref.shape[1]
    ti = topi_ref[...]
    e_i = lax.broadcasted_iota(I32, (e_pad, tm), 0)
    hits = [e_i == ti[kk:kk + 1, :] for kk in range(TOP_K)]
    sel = hits[0] | hits[1] | hits[2] | hits[3]
    s_i = lax.broadcasted_iota(I32, (tm, tm), 0)
    t_i = lax.broadcasted_iota(I32, (tm, tm), 1)
    before = (s_i < t_i).astype(BF16)
    cum = _dot(sel.astype(BF16), before) + off_ref[0]
    k8 = lax.broadcasted_iota(I32, (SUBLANES, tm), 0)
    pos = jnp.zeros((SUBLANES, tm), F32)
    for kk in range(TOP_K):
        pk = jnp.sum(jnp.where(hits[kk], cum, 0.0), axis=0, keepdims=True)
        pos = jnp.where(k8 == kk, pk, pos)
    pos_ref[...] = pos.astype(I32)


def _positions(topi, blk_off, e_pad):
    n = topi.shape[1]
    tm = POS_TM
    return pl.pallas_call(
        functools.partial(_pos_kernel, e_pad=e_pad),
        grid=(n // tm,),
        in_specs=[pl.BlockSpec((SUBLANES, tm), lambda i: (0, i)),
                  pl.BlockSpec((1, e_pad, 1), lambda i: (i, 0, 0))],
        out_specs=pl.BlockSpec((SUBLANES, tm), lambda i: (0, i)),
        out_shape=jax.ShapeDtypeStruct((SUBLANES, n), I32),
        compiler_params=_params(1),
    )(topi, blk_off)


def _blocked_pos(pos, tm):
    n = pos.shape[1]
    p = pos[:TOP_K].reshape(TOP_K, n // tm, tm)
    return jnp.transpose(p, (1, 0, 2)).reshape(n // tm, 1, TOP_K * tm)


def _dispatch_kernel(fill_ref, pos_ref, x_ref, xs_ref, zero_ref, sem, fill_sem, *, chunks):
    tm = x_ref.shape[0] // chunks
    tb = zero_ref.shape[0] // chunks

    def rows_of(ref, first, count):
        return ref.at[pl.ds(pl.multiple_of(first * chunks, chunks), count * chunks)]

    @pl.when(pl.program_id(0) == 0)
    def _():
        zero_ref[...] = jnp.zeros_like(zero_ref)

        def each_copy(b, action):
            cnt = fill_ref[b]
            row = (b + 1) * tb - cnt
            bit = tb
            while bit:
                @pl.when((cnt & bit) != 0)
                def _(row=row, bit=bit):
                    action(pltpu.make_async_copy(rows_of(zero_ref, 0, bit), rows_of(xs_ref, row, bit), fill_sem))

                row = row + (cnt & bit)
                bit //= 2

        def start_all(b, carry):
            each_copy(b, lambda cp: cp.start())
            return carry

        def wait_all(b, carry):
            each_copy(b, lambda cp: cp.wait())
            return carry

        lax.fori_loop(0, fill_ref.shape[0], start_all, 0)
        lax.fori_loop(0, fill_ref.shape[0], wait_all, 0)

    def issue(g, carry):
        for u in range(DMA_UNROLL):
            t = pl.multiple_of(g * DMA_UNROLL, DMA_UNROLL) + u
            for kk in range(TOP_K):
                p = pos_ref[0, 0, kk * tm + t]
                pltpu.make_async_copy(rows_of(x_ref, t, 1), rows_of(xs_ref, p, 1), sem).start()
        return carry

    lax.fori_loop(0, tm // DMA_UNROLL, issue, 0)
    for _ in range(TOP_K):
        pltpu.make_async_copy(x_ref, rows_of(xs_ref, 0, tm), sem).wait()


def _dispatch(blk_fill, pos_blocked, x1p, n_rows, chunks):
    n = x1p.shape[0] // chunks
    tm = DISPATCH_TM
    grid_spec = pltpu.PrefetchScalarGridSpec(
        num_scalar_prefetch=1,
        grid=(n // tm,),
        in_specs=[pl.BlockSpec((1, 1, TOP_K * tm), lambda i, fl: (i, 0, 0), memory_space=pltpu.SMEM),
                  pl.BlockSpec((tm * chunks, LANES), lambda i, fl: (i, 0))],
        out_specs=pl.BlockSpec(memory_space=pl.ANY),
        scratch_shapes=[pltpu.VMEM((EXPERT_ROWS * chunks, LANES), U32), pltpu.SemaphoreType.DMA(()),
                        pltpu.SemaphoreType.DMA(())],
    )
    return pl.pallas_call(
        functools.partial(_dispatch_kernel, chunks=chunks),
        grid_spec=grid_spec,
        out_shape=jax.ShapeDtypeStruct((n_rows * chunks, LANES), U32),
        compiler_params=_params(1),
    )(blk_fill, pos_blocked, x1p)


def _expert_up_kernel(blk_ref, col_ref, exp_ref, flag_ref, xs_ref, wg_ref, wu_ref, bg_ref, bu_ref, act_ref,
                      wgb_ref, wub_ref):
    del blk_ref, col_ref, exp_ref
    flag = flag_ref[pl.program_id(0)]

    @pl.when((flag & 2) != 0)
    def _():
        wgb_ref[...] = wg_ref[...].astype(BF16)
        wub_ref[...] = wu_ref[...].astype(BF16)

    @pl.when((flag & 1) != 0)
    def _():
        halves = [_unpack_halves(u) for u in _load_row_tiles(xs_ref, act_ref.shape[0])]
        xb = jnp.concatenate([lo.astype(BF16) for lo, _ in halves] + [hi.astype(BF16) for _, hi in halves], axis=1)
        for c in range(0, act_ref.shape[1], MXU_COLS):
            cs = slice(c, c + MXU_COLS)
            g = jnp.minimum(_dot(xb, wgb_ref[:, cs]) + bg_ref[:, cs], SWIGLU_LIMIT)
            u = jnp.clip(_dot(xb, wub_ref[:, cs]) + bu_ref[:, cs], -SWIGLU_LIMIT, SWIGLU_LIMIT)
            act_ref[:, cs] = ((u + 1.0) * (g * jax.nn.sigmoid(SWIGLU_ALPHA * g))).astype(act_ref.dtype)

    @pl.when((flag & 1) == 0)
    def _():
        act_ref[...] = jnp.zeros_like(act_ref)


def _expert_down_kernel(be_ref, nu_ref, first_ref, act_ref, wd_ref, bd_ref, ys_ref, wdb_ref):
    del be_ref
    i = pl.program_id(0)

    @pl.when(first_ref[i] != 0)
    def _():
        wdb_ref[...] = wd_ref[...].astype(BF16)

    @pl.when(i < nu_ref[0])
    def _():
        act = act_ref[...]
        half = wdb_ref.shape[1] // 2
        for c in range(0, half, MXU_COLS):
            lo = _dot(act, wdb_ref[:, c:c + MXU_COLS]) + bd_ref[:, c:c + MXU_COLS]
            hi = _dot(act, wdb_ref[:, half + c:half + c + MXU_COLS]) + bd_ref[:, half + c:half + c + MXU_COLS]
            _store_row_tiles(ys_ref, _pack_halves(jnp.concatenate([lo, hi], axis=1)), first=c // LANES)

    @pl.when(i >= nu_ref[0])
    def _():
        ys_ref[...] = jnp.zeros_like(ys_ref)


def _experts(layer, sched, xs, w_gu, b_gu, w_dn, b_dn):
    _, e, d, f2 = w_gu.shape
    chunks = d // 2 // LANES
    n_rows = xs.shape[0] // chunks
    f = f2 // 2
    tb, fh = EXPERT_ROWS, EXPERT_UP_COLS
    nh = f // fh
    nb = n_rows // tb
    up_spec = pltpu.PrefetchScalarGridSpec(
        num_scalar_prefetch=4,
        grid=(nh * nb,),
        in_specs=[pl.BlockSpec((tb * chunks, LANES), lambda t, bl, co, ex, fl: (bl[t], 0)),
                  pl.BlockSpec((None, None, d, fh), lambda t, bl, co, ex, fl: (layer, ex[t], 0, co[t])),
                  pl.BlockSpec((None, None, d, fh), lambda t, bl, co, ex, fl: (layer, ex[t], 0, nh + co[t])),
                  pl.BlockSpec((None, None, 1, fh), lambda t, bl, co, ex, fl: (layer, ex[t], 0, co[t])),
                  pl.BlockSpec((None, None, 1, fh), lambda t, bl, co, ex, fl: (layer, ex[t], 0, nh + co[t]))],
        out_specs=pl.BlockSpec((tb, fh), lambda t, bl, co, ex, fl: (bl[t], co[t])),
        scratch_shapes=[pltpu.VMEM((d, fh), BF16), pltpu.VMEM((d, fh), BF16)],
    )
    b_gu4 = b_gu.reshape(b_gu.shape[0], e, 1, f2)
    act = pl.pallas_call(
        _expert_up_kernel,
        grid_spec=up_spec,
        out_shape=jax.ShapeDtypeStruct((n_rows, f), BF16),
        compiler_params=_params(1),
    )(sched["it_blk"], sched["it_col"], sched["it_exp"], sched["it_flag"], xs, w_gu, w_gu, b_gu4, b_gu4)

    def blk(i, nu):
        return jnp.minimum(i, nu[0] - 1)

    down_spec = pltpu.PrefetchScalarGridSpec(
        num_scalar_prefetch=3,
        grid=(nb,),
        in_specs=[pl.BlockSpec((tb, f), lambda i, be, nu, fi: (blk(i, nu), 0)),
                  pl.BlockSpec((None, None, f, d), lambda i, be, nu, fi: (layer, be[i], 0, 0)),
                  pl.BlockSpec((None, None, 1, d), lambda i, be, nu, fi: (layer, be[i], 0, 0))],
        out_specs=pl.BlockSpec((tb * chunks, LANES), lambda i, be, nu, fi: (i, 0)),
        scratch_shapes=[pltpu.VMEM((f, d), BF16)],
    )
    return pl.pallas_call(
        _expert_down_kernel,
        grid_spec=down_spec,
        out_shape=jax.ShapeDtypeStruct((n_rows * chunks, LANES), U32),
        compiler_params=_params(1),
    )(sched["blk_e"], sched["n_used"], sched["blk_first"], act, w_dn, b_dn.reshape(b_dn.shape[0], e, 1, d))


def _expert_schedule(total, n_rows, e, f):
    tb = EXPERT_ROWS
    nh = f // EXPERT_UP_COLS
    nb = n_rows // tb
    padded = (total + tb - 1) // tb * tb
    pad_end = jnp.cumsum(padded)
    pad_start = pad_end - padded
    n_used = pad_end[-1] // tb
    blocks = jnp.arange(nb, dtype=I32)
    blk_e = jnp.minimum(jnp.sum(pad_end[None, :] <= (blocks * tb)[:, None], axis=1), e - 1).astype(I32)
    blk_e = jnp.where(blocks < n_used, blk_e, blk_e[n_used - 1])
    blk_first = jnp.concatenate([jnp.ones((1,), I32), (blk_e[1:] != blk_e[:-1]).astype(I32)])
    grp_last = blocks == pad_end[blk_e] // tb - 1
    blk_fill = jnp.where(blocks < n_used, jnp.where(grp_last, (padded - total)[blk_e], 0), tb).astype(I32)
    items = jnp.arange(nh * nb, dtype=I32)
    grp_end = nh * (pad_end // tb)
    it_e = jnp.minimum(jnp.sum(grp_end[None, :] <= items[:, None], axis=1), e - 1).astype(I32)
    grp_blocks = jnp.maximum((padded // tb)[it_e], 1)
    r = items - nh * (pad_start // tb)[it_e]
    valid = items < nh * n_used
    spare = items - nh * n_used
    it_blk = jnp.where(valid, (pad_start // tb)[it_e] + r % grp_blocks, n_used + spare // nh)
    it_col = jnp.where(valid, r // grp_blocks, spare % nh)
    it_exp = jnp.where(valid, it_e, blk_e[n_used - 1])
    it_flag = jnp.where(valid, 1 + 2 * (r % grp_blocks == 0).astype(I32), 0)
    it_flag = it_flag.at[0].set(it_flag[0] | 2)
    return dict(pad_start=pad_start, blk_fill=blk_fill, blk_e=blk_e, n_used=n_used.astype(I32).reshape(1),
                blk_first=blk_first, it_blk=it_blk.astype(I32), it_col=it_col.astype(I32),
                it_exp=it_exp.astype(I32), it_flag=it_flag.astype(I32))


def _combine_kernel(pos_ref, nxt_ref, ys_ref, x1_ref, gcol_ref, p_ref, g_ref, b_ref, wpg_ref, wpp_ref, out_ref,
                    buf_ref, sem, *, alpha):
    tm = x1_ref.shape[0]
    chunks = buf_ref.shape[2] // tm
    i = pl.program_id(0)
    slot = i % 2

    def gather(rows_ref, dst):
        def issue(g, carry):
            for u in range(DMA_UNROLL):
                t = pl.multiple_of(g * DMA_UNROLL, DMA_UNROLL) + u
                for kk in range(TOP_K):
                    r = rows_ref[0, 0, kk * tm + t]
                    pltpu.make_async_copy(ys_ref.at[pl.ds(pl.multiple_of(r * chunks, chunks), chunks)],
                                          buf_ref.at[dst, kk, pl.ds(pl.multiple_of(t * chunks, chunks), chunks)],
                                          sem.at[dst]).start()
            return carry

        lax.fori_loop(0, tm // DMA_UNROLL, issue, 0)

    @pl.when(i == 0)
    def _():
        gather(pos_ref, 0)

    @pl.when(i + 1 < pl.num_programs(0))
    def _():
        gather(nxt_ref, 1 - slot)

    for kk in range(TOP_K):
        pltpu.make_async_copy(ys_ref.at[pl.ds(0, tm * chunks)], buf_ref.at[slot, kk], sem.at[slot]).wait()

    gates = gcol_ref[...]
    lo = hi = None
    for kk in range(TOP_K):
        gk = gates[:, kk:kk + 1]
        halves = [_unpack_halves(u) for u in _load_row_tiles(buf_ref, tm, (slot, kk))]
        lo = [gk * l for l, _ in halves] if lo is None else [a + gk * l for a, (l, _) in zip(lo, halves)]
        hi = [gk * h for _, h in halves] if hi is None else [a + gk * h for a, (_, h) in zip(hi, halves)]
    ffn = jnp.concatenate(lo + hi, axis=1)
    x2 = _layer_norm(alpha * x1_ref[...] + ffn, g_ref[...], b_ref[...])
    gate = jax.nn.sigmoid(_dot(x2.astype(BF16), wpg_ref[...]))
    out_ref[...] = x2 + gate * _dot(p_ref[...].astype(BF16), wpp_ref[...])


def _combine(pos_blocked, ys, x1, gcol, p, ln_g, ln_b, w_pg, w_pp, alpha):
    n, d = x1.shape
    tm = COMBINE_TM
    pd = p.shape[1]
    full = lambda i: (0, 0)
    once = pl.Buffered(1)
    last = n // tm - 1
    return pl.pallas_call(
        functools.partial(_combine_kernel, alpha=alpha),
        grid=(n // tm,),
        in_specs=[pl.BlockSpec((1, 1, TOP_K * tm), lambda i: (i, 0, 0), memory_space=pltpu.SMEM),
                  pl.BlockSpec((1, 1, TOP_K * tm), lambda i: (jnp.minimum(i + 1, last), 0, 0),
                               memory_space=pltpu.SMEM),
                  pl.BlockSpec(memory_space=pl.ANY),
                  pl.BlockSpec((tm, d), lambda i: (i, 0)),
                  pl.BlockSpec((tm, LANES), lambda i: (i, 0)),
                  pl.BlockSpec((tm, pd), lambda i: (i, 0)),
                  pl.BlockSpec((1, d), full),
                  pl.BlockSpec((1, d), full),
                  pl.BlockSpec((d, d), full, pipeline_mode=once),
                  pl.BlockSpec((pd, d), full, pipeline_mode=once)],
        out_specs=pl.BlockSpec((tm, d), lambda i: (i, 0)),
        out_shape=jax.ShapeDtypeStruct((n, d), F32),
        scratch_shapes=[pltpu.VMEM((2, TOP_K, tm * (d // 2 // LANES), LANES), U32), pltpu.SemaphoreType.DMA((2,))],
        compiler_params=_params(1),
    )(pos_blocked, pos_blocked, ys, x1, gcol, p, ln_g.reshape(1, d), ln_b.reshape(1, d), w_pg, w_pp)


def _moe_tail(layer, x1, x1p, topi, gcol, cnt, p, ln_g, ln_b, w_gu, b_gu, w_dn, b_dn, w_pg, w_pp, alpha):
    n, d = x1.shape
    e = w_gu.shape[1]
    e_pad = cnt.shape[1]
    tb = EXPERT_ROWS
    c = cnt[:, :, 0].astype(I32)
    c = c.reshape(n // POS_TM, POS_TM // POST_TM, e_pad).sum(axis=1)
    n_rows = -(-n * TOP_K // tb) * tb + e * tb
    sched = _expert_schedule(c.sum(axis=0), n_rows, e, w_gu.shape[3] // 2)
    blk_off = (sched["pad_start"][None, :] + jnp.cumsum(c, axis=0) - c).astype(F32)[:, :, None]

    pos = _positions(topi, blk_off, e_pad)
    xs = _dispatch(sched["blk_fill"], _blocked_pos(pos, DISPATCH_TM), x1p, n_rows, d // 2 // LANES)
    ys = _experts(layer, sched, xs, w_gu, b_gu, w_dn, b_dn)
    return _combine(_blocked_pos(pos, COMBINE_TM), ys, x1, gcol, p, ln_g, ln_b, w_pg, w_pp, alpha)


def kernel(x, p, ln_g, ln_b, w_in_a, b_gate_a, norm_a, w_out_a, w_in_b, conv_b, w_out_b, w_router, b_router,
           w_gu, b_gu, w_dn, b_dn, w_ple_gate, w_ple_proj):
    bsz, seq, d = x.shape
    depth = ln_g.shape[0]
    n = bsz * seq
    alpha = (2 * depth) ** 0.25
    heads = b_gate_a.shape[-1] // 2
    dv = d // heads
    dk = dv // 2
    qkvo = 2 * heads * dk + 2 * heads * dv
    xf = x.reshape(n, d)
    for i in range(depth):
        j = i // 2
        if i % 2 == 0:
            col, row = _mlstm_gates(xf, w_in_a[j][:, qkvo:], b_gate_a[j], heads)
            proj = _proj(xf, w_in_a[j][:, :qkvo].astype(BF16), PROJ_TM, PROJ_TN_A)
            a = _mlstm(proj, col, row, norm_a[j], bsz, seq, heads, dk, dv)
            w_out = w_out_a[j]
        else:
            a = _conv_front(xf, w_in_b[j].astype(BF16), conv_b[j], seq)
            w_out = w_out_b[j]
        x1, x1p, topi, gcol, cnt = _post_mixer(a, w_out.astype(BF16), xf, ln_g[i, 0], ln_b[i, 0],
                                               w_router[i], b_router[i], alpha)
        xf = _moe_tail(i, x1, x1p, topi, gcol, cnt, p[i].reshape(n, -1), ln_g[i, 1], ln_b[i, 1],
                       w_gu, b_gu, w_dn, b_dn,
                       w_ple_gate[i].astype(BF16), w_ple_proj[i].astype(BF16), alpha)
    return xf.reshape(bsz, seq, d)
```

```python
import functools

import jax
import jax.numpy as jnp
from jax import lax
from jax.experimental import pallas as pl
from jax.experimental.pallas import tpu as pltpu

F32 = jnp.float32
BF16 = jnp.bfloat16
U32 = jnp.uint32
I32 = jnp.int32

GATE_SOFTCAP = 15.0
CONV_WIDTH = 3
TOP_K = 4
SWIGLU_ALPHA = 1.702
SWIGLU_LIMIT = 7.0
LN_EPS = 1e-5
RMS_EPS = 1e-6

LANES = 128
SUBLANES = 8
MXU_COLS = 256
VMEM_LIMIT_BYTES = 60000 * 1024

MLSTM_ROWS = 256
PROJ_TM = 1024
PROJ_TN_A = 1024
PROJ_TN_B = 512
POST_TM = 512
POS_TM = 512
DISPATCH_TM = 1024
EXPERT_ROWS = 512
EXPERT_UP_COLS = 1024
COMBINE_TM = 256
DMA_UNROLL = 8


def _dot(a, b):
    return jnp.dot(a, b, preferred_element_type=F32)


def _dot_nt(a, b):
    return lax.dot_general(a, b, (((1,), (1,)), ((), ())), preferred_element_type=F32)


def _dot_tn(a, b):
    return lax.dot_general(a, b, (((0,), (0,)), ((), ())), preferred_element_type=F32)


def _split2(x):
    hi = x.astype(BF16)
    lo = (x - hi.astype(F32)).astype(BF16)
    return hi, lo


def _split3(x):
    hi = x.astype(BF16)
    r = x - hi.astype(F32)
    mid = r.astype(BF16)
    lo = (r - mid.astype(F32)).astype(BF16)
    return hi, mid, lo


def _pack_halves(y):
    half = y.shape[1] // 2
    lo = lax.bitcast_convert_type(y[:, :half].astype(BF16).astype(F32), U32) >> 16
    hi = lax.bitcast_convert_type(y[:, half:].astype(BF16).astype(F32), U32) & jnp.uint32(0xFFFF0000)
    return lo | hi


def _unpack_halves(u):
    lo = lax.bitcast_convert_type(u << 16, F32)
    hi = lax.bitcast_convert_type(u & jnp.uint32(0xFFFF0000), F32)
    return lo, hi


def _store_row_tiles(ref, words, first=0):
    r = words.shape[0]
    c = ref.shape[0] // r
    for j in range(words.shape[1] // LANES):
        ref[pl.ds(first + j, r, stride=c), :] = words[:, j * LANES:(j + 1) * LANES]


def _load_row_tiles(ref, r, index=()):
    c = ref.shape[-2] // r
    return [ref[index + (pl.ds(j, r, stride=c), slice(None))] for j in range(c)]


def _layer_norm(y, g, b):
    mu = jnp.mean(y, axis=-1, keepdims=True)
    yc = y - mu
    var = jnp.mean(yc * yc, axis=-1, keepdims=True)
    return yc * lax.rsqrt(var + LN_EPS) * g + b


def _params(n_axes):
    return pltpu.CompilerParams(dimension_semantics=("arbitrary",) * n_axes,
                                vmem_limit_bytes=VMEM_LIMIT_BYTES)


def _gate_kernel(x_ref, w_ref, b_ref, col_ref, row_ref, *, heads):
    rows = x_ref.shape[0]
    xh, xl = _split2(x_ref[...])
    wh, wl = _split2(w_ref[...])
    g = _dot(xh, wh) + _dot(xh, wl) + _dot(xl, wh) + b_ref[...]
    g = GATE_SOFTCAP * jnp.tanh(g / GATE_SOFTCAP)
    lf = -jnp.log1p(jnp.exp(-g))
    r_i = lax.broadcasted_iota(I32, (rows, rows), 0)
    c_i = lax.broadcasted_iota(I32, (rows, rows), 1)
    tril = (c_i <= r_i).astype(BF16)
    p0, p1, p2 = _split3(lf)
    bsum = _dot(tril, p0) + _dot(tril, p1) + _dot(tril, p2)
    lane = lax.broadcasted_iota(I32, g.shape, 1)
    col = jnp.where(lane < heads, g, jnp.where(lane < 2 * heads, bsum, 0.0))
    col_ref[...] = col
    row_ref[...] = col.T[: row_ref.shape[0]]


def _mlstm_gates(xf, w_g, b_g, heads):
    n, d = xf.shape
    rows = MLSTM_ROWS
    r_pad = -(-2 * heads // SUBLANES) * SUBLANES
    wg = jnp.zeros((d, LANES), F32).at[:, : 2 * heads].set(w_g)
    bg = jnp.zeros((1, LANES), F32).at[0, : 2 * heads].set(b_g)
    return pl.pallas_call(
        functools.partial(_gate_kernel, heads=heads),
        grid=(n // rows,),
        in_specs=[pl.BlockSpec((rows, d), lambda i: (i, 0)),
                  pl.BlockSpec((d, LANES), lambda i: (0, 0)),
                  pl.BlockSpec((1, LANES), lambda i: (0, 0))],
        out_specs=[pl.BlockSpec((rows, LANES), lambda i: (i, 0)),
                   pl.BlockSpec((r_pad, rows), lambda i: (0, i))],
        out_shape=[jax.ShapeDtypeStruct((n, LANES), F32),
                   jax.ShapeDtypeStruct((r_pad, n), F32)],
        compiler_params=_params(1),
    )(xf, wg, bg)


def _proj_kernel(x_ref, w_ref, o_ref, xb_ref):
    @pl.when(pl.program_id(1) == 0)
    def _():
        xb_ref[...] = x_ref[...].astype(BF16)

    o_ref[...] = _dot(xb_ref[...], w_ref[...]).astype(o_ref.dtype)


def _proj(xf, w, tm, tn):
    n, d = xf.shape
    m = w.shape[1]
    while m % tn:
        tn //= 2
    return pl.pallas_call(
        _proj_kernel,
        grid=(n // tm, m // tn),
        in_specs=[pl.BlockSpec((tm, d), lambda i, j: (i, 0)),
                  pl.BlockSpec((d, tn), lambda i, j: (0, j))],
        out_specs=pl.BlockSpec((tm, tn), lambda i, j: (i, j)),
        out_shape=jax.ShapeDtypeStruct((n, m), BF16),
        scratch_shapes=[pltpu.VMEM((tm, d), BF16)],
        compiler_params=_params(2),
    )(xf, w)


def _mlstm_kernel(q_ref, k_ref, v_ref, o_ref, col_ref, row_ref, norm_ref, out_ref, c_ref, n_ref, *, heads, dk, dv):
    rows = q_ref.shape[0]
    scale = dk ** -0.5

    @pl.when(pl.program_id(1) == 0)
    def _():
        c_ref[...] = jnp.zeros_like(c_ref)
        n_ref[...] = jnp.zeros_like(n_ref)

    col = col_ref[...]
    row = row_ref[...]
    t_i = lax.broadcasted_iota(I32, (rows, rows), 0)
    s_i = lax.broadcasted_iota(I32, (rows, rows), 1)
    causal = s_i <= t_i
    for h in range(heads):
        q = q_ref[:, h * dk:(h + 1) * dk]
        k = k_ref[:, h * dk:(h + 1) * dk]
        v = v_ref[:, h * dv:(h + 1) * dv]
        li_c = col[:, h:h + 1]
        b_c = col[:, heads + h:heads + h + 1]
        li_r = row[h:h + 1, :]
        b_r = row[heads + h:heads + h + 1, :]
        b_last = b_c[rows - 1:rows, :]
        decay_w = jnp.where(causal, jnp.exp(b_c - b_r + li_r), 0.0)
        sw = _dot_nt(q, k) * scale * decay_w
        eb = jnp.exp(b_c) * scale
        c_prev = c_ref[h]
        n_prev = n_ref[h]
        num = _dot(sw.astype(BF16), v) + eb * _dot(q, c_prev.astype(BF16))
        qn = jnp.sum(q.astype(F32) * n_prev, axis=-1, keepdims=True)
        den = jnp.sum(sw, axis=-1, keepdims=True) + eb * qn
        hh = num / jnp.maximum(jnp.abs(den), 1.0)
        wk = jnp.exp(b_last - b_c + li_c)
        carry = jnp.exp(b_last)
        kf = k.astype(F32) * wk
        c_ref[h] = carry * c_prev + _dot_tn(kf.astype(BF16), v)
        n_ref[h] = carry * n_prev + jnp.sum(kf, axis=0, keepdims=True)
        hh = hh * lax.rsqrt(jnp.mean(hh * hh, axis=-1, keepdims=True) + RMS_EPS)
        hh = hh * norm_ref[:, h * dv:(h + 1) * dv]
        og = o_ref[:, h * dv:(h + 1) * dv].astype(F32)
        out_ref[:, h * dv:(h + 1) * dv] = (hh * jax.nn.sigmoid(og)).astype(out_ref.dtype)


def _mlstm(proj, col, row, norm, bsz, seq, heads, dk, dv):
    rows = MLSTM_ROWS
    nc = seq // rows
    qk_w, v_w = heads * dk, heads * dv
    assert v_w == 2 * qk_w
    r_pad = row.shape[0]
    tok = lambda b, c: b * nc + c
    return pl.pallas_call(
        functools.partial(_mlstm_kernel, heads=heads, dk=dk, dv=dv),
        grid=(bsz, nc),
        in_specs=[pl.BlockSpec((rows, qk_w), lambda b, c: (tok(b, c), 0)),
                  pl.BlockSpec((rows, qk_w), lambda b, c: (tok(b, c), 1)),
                  pl.BlockSpec((rows, v_w), lambda b, c: (tok(b, c), 1)),
                  pl.BlockSpec((rows, v_w), lambda b, c: (tok(b, c), 2)),
                  pl.BlockSpec((rows, LANES), lambda b, c: (tok(b, c), 0)),
                  pl.BlockSpec((r_pad, rows), lambda b, c: (0, tok(b, c))),
                  pl.BlockSpec((1, v_w), lambda b, c: (0, 0))],
        out_specs=pl.BlockSpec((rows, v_w), lambda b, c: (tok(b, c), 0)),
        out_shape=jax.ShapeDtypeStruct((bsz * seq, v_w), BF16),
        scratch_shapes=[pltpu.VMEM((heads, dk, dv), F32), pltpu.VMEM((heads, 1, dk), F32)],
        compiler_params=_params(2),
    )(proj, proj, proj, proj, col, row, norm.reshape(1, v_w))


def _conv_kernel(x_ref, wb_ref, wc_ref, wu_ref, cw_ref, o_ref, xb_ref, carry_ref, *, steps_per_seq):
    i, j = pl.program_id(0), pl.program_id(1)
    tm = x_ref.shape[0]

    @pl.when(j == 0)
    def _():
        xb_ref[...] = x_ref[...].astype(BF16)

    xb = xb_ref[...]
    z = _dot(xb, wc_ref[...]) * _dot(xb, wu_ref[...])
    @pl.when(i % steps_per_seq == 0)
    def _():
        carry_ref[j] = jnp.zeros(carry_ref.shape[1:], F32)

    prev = carry_ref[j]
    carry_ref[j] = z[tm - SUBLANES:, :]
    r_i = lax.broadcasted_iota(I32, z.shape, 0)
    z1 = jnp.where(r_i == 0, prev[SUBLANES - 1:SUBLANES, :], pltpu.roll(z, 1, 0))
    z2 = jnp.where(r_i == 0, prev[SUBLANES - 2:SUBLANES - 1, :],
                   jnp.where(r_i == 1, prev[SUBLANES - 1:SUBLANES, :], pltpu.roll(z, 2, 0)))
    cw = cw_ref[...]
    zc = cw[0:1, :] * z2 + cw[1:2, :] * z1 + cw[2:3, :] * z
    o_ref[...] = (_dot(xb, wb_ref[...]) * zc).astype(o_ref.dtype)


def _conv_front(xf, w_in, conv_w, seq):
    n, d = xf.shape
    tm, tn = PROJ_TM, PROJ_TN_B
    nj = d // tn
    cw = jnp.zeros((SUBLANES, d), F32).at[:CONV_WIDTH].set(conv_w)
    return pl.pallas_call(
        functools.partial(_conv_kernel, steps_per_seq=seq // tm),
        grid=(n // tm, nj),
        in_specs=[pl.BlockSpec((tm, d), lambda i, j: (i, 0)),
                  pl.BlockSpec((d, tn), lambda i, j: (0, j)),
                  pl.BlockSpec((d, tn), lambda i, j: (0, nj + j)),
                  pl.BlockSpec((d, tn), lambda i, j: (0, 2 * nj + j)),
                  pl.BlockSpec((SUBLANES, tn), lambda i, j: (0, j))],
        out_specs=pl.BlockSpec((tm, tn), lambda i, j: (i, j)),
        out_shape=jax.ShapeDtypeStruct((n, d), BF16),
        scratch_shapes=[pltpu.VMEM((tm, d), BF16), pltpu.VMEM((nj, SUBLANES, tn), F32)],
        compiler_params=_params(2),
    )(xf, w_in, w_in, w_in, cw)


def _post_kernel(a_ref, w_ref, x_ref, g_ref, b_ref, wr_ref, br_ref,
                 x1_ref, x1p_ref, topi_ref, gcol_ref, cnt_ref, *, alpha, e_pad):
    tm = a_ref.shape[0]
    y = alpha * x_ref[...] + _dot(a_ref[...], w_ref[...])
    x1 = _layer_norm(y, g_ref[...], b_ref[...])
    x1_ref[...] = x1
    _store_row_tiles(x1p_ref, _pack_halves(x1))
    xh, xl = _split2(x1)
    w2 = wr_ref[...]
    l2 = _dot(xh, w2)
    logits = l2[:, :LANES] + l2[:, LANES:] + _dot(xl, w2[:, :LANES]) + br_ref[...]
    lt = logits.T[:e_pad]
    e_i = lax.broadcasted_iota(I32, lt.shape, 0)
    cur = lt
    ids, vals = [], []
    for _ in range(TOP_K):
        m = jnp.max(cur, axis=0, keepdims=True)
        idx = jnp.min(jnp.where(cur == m, e_i, e_pad), axis=0, keepdims=True)
        cur = jnp.where(e_i == idx, -jnp.inf, cur)
        ids.append(idx)
        vals.append(m)
    ex = [jnp.exp(v - vals[0]) for v in vals]
    den = ex[0] + ex[1] + ex[2] + ex[3]
    k8 = lax.broadcasted_iota(I32, (SUBLANES, tm), 0)
    k128 = lax.broadcasted_iota(I32, (LANES, tm), 0)
    topi = jnp.zeros((SUBLANES, tm), I32)
    gmat = jnp.zeros((LANES, tm), F32)
    for kk in range(TOP_K):
        topi = jnp.where(k8 == kk, ids[kk], topi)
        gmat = jnp.where(k128 == kk, ex[kk] / den, gmat)
    topi_ref[...] = topi
    gcol_ref[...] = gmat.T
    sel = (cur == -jnp.inf).astype(F32)
    cnt_ref[0] = jnp.broadcast_to(jnp.sum(sel, axis=1, keepdims=True), (e_pad, LANES))


def _post_mixer(a, w_out, xf, ln_g, ln_b, w_router, b_router, alpha):
    n, d = xf.shape
    tm = POST_TM
    e = w_router.shape[1]
    assert TOP_K <= e <= LANES
    e_pad = -(-e // SUBLANES) * SUBLANES
    wr = jnp.zeros((d, LANES), F32).at[:, :e].set(w_router)
    wr_hi = wr.astype(BF16)
    wr_lo = (wr - wr_hi.astype(F32)).astype(BF16)
    wr2 = jnp.concatenate([wr_hi, wr_lo], axis=1)
    br = jnp.full((1, LANES), -1e30, F32).at[0, :e].set(b_router)
    full = lambda i: (0, 0)
    once = pl.Buffered(1)
    chunks = d // 2 // LANES
    return pl.pallas_call(
        functools.partial(_post_kernel, alpha=alpha, e_pad=e_pad),
        grid=(n // tm,),
        in_specs=[pl.BlockSpec((tm, d), lambda i: (i, 0)),
                  pl.BlockSpec((d, d), full, pipeline_mode=once),
                  pl.BlockSpec((tm, d), lambda i: (i, 0)),
                  pl.BlockSpec((1, d), full),
                  pl.BlockSpec((1, d), full),
                  pl.BlockSpec((d, 2 * LANES), full, pipeline_mode=once),
                  pl.BlockSpec((1, LANES), full)],
        out_specs=[pl.BlockSpec((tm, d), lambda i: (i, 0)),
                   pl.BlockSpec((tm * chunks, LANES), lambda i: (i, 0)),
                   pl.BlockSpec((SUBLANES, tm), lambda i: (0, i)),
                   pl.BlockSpec((tm, LANES), lambda i: (i, 0)),
                   pl.BlockSpec((1, e_pad, LANES), lambda i: (i, 0, 0))],
        out_shape=[jax.ShapeDtypeStruct((n, d), F32),
                   jax.ShapeDtypeStruct((n * chunks, LANES), U32),
                   jax.ShapeDtypeStruct((SUBLANES, n), I32),
                   jax.ShapeDtypeStruct((n, LANES), F32),
                   jax.ShapeDtypeStruct((n // tm, e_pad, LANES), F32)],
        compiler_params=_params(1),
    )(a, w_out, xf, ln_g.reshape(1, d), ln_b.reshape(1, d), wr2, br)


def _pos_kernel(topi_ref, off_ref, pos_ref, *, e_pad):
    tm = topi_ref.shape[1]
    ti = topi_ref[...]
    e_i = lax.broadcasted_iota(I32, (e_pad, tm), 0)
    hits = [e_i == ti[kk:kk + 1, :] for kk in range(TOP_K)]
    sel = hits[0] | hits[1] | hits[2] | hits[3]
    s_i = lax.broadcasted_iota(I32, (tm, tm), 0)
    t_i = lax.broadcasted_iota(I32, (tm, tm), 1)
    before = (s_i < t_i).astype(BF16)
    cum = _dot(sel.astype(BF16), before) + off_ref[0]
    k8 = lax.broadcasted_iota(I32, (SUBLANES, tm), 0)
    pos = jnp.zeros((SUBLANES, tm), F32)
    for kk in range(TOP_K):
        pk = jnp.sum(jnp.where(hits[kk], cum, 0.0), axis=0, keepdims=True)
        pos = jnp.where(k8 == kk, pk, pos)
    pos_ref[...] = pos.astype(I32)


def _positions(topi, blk_off, e_pad):
    n = topi.shape[1]
    tm = POS_TM
    return pl.pallas_call(
        functools.partial(_pos_kernel, e_pad=e_pad),
        grid=(n // tm,),
        in_specs=[pl.BlockSpec((SUBLANES, tm), lambda i: (0, i)),
                  pl.BlockSpec((1, e_pad, 1), lambda i: (i, 0, 0))],
        out_specs=pl.BlockSpec((SUBLANES, tm), lambda i: (0, i)),
        out_shape=jax.ShapeDtypeStruct((SUBLANES, n), I32),
        compiler_params=_params(1),
    )(topi, blk_off)


def _blocked_pos(pos, tm):
    n = pos.shape[1]
    p = pos[:TOP_K].reshape(TOP_K, n // tm, tm)
    return jnp.transpose(p, (1, 0, 2)).reshape(n // tm, 1, TOP_K * tm)


def _dispatch_kernel(fill_ref, pos_ref, x_ref, xs_ref, zero_ref, sem, fill_sem, *, chunks):
    tm = x_ref.shape[0] // chunks
    tb = zero_ref.shape[0] // chunks

    def rows_of(ref, first, count):
        return ref.at[pl.ds(pl.multiple_of(first * chunks, chunks), count * chunks)]

    @pl.when(pl.program_id(0) == 0)
    def _():
        zero_ref[...] = jnp.zeros_like(zero_ref)

        def each_copy(b, action):
            cnt = fill_ref[b]
            row = (b + 1) * tb - cnt
            bit = tb
            while bit:
                @pl.when((cnt & bit) != 0)
                def _(row=row, bit=bit):
                    action(pltpu.make_async_copy(rows_of(zero_ref, 0, bit), rows_of(xs_ref, row, bit), fill_sem))

                row = row + (cnt & bit)
                bit //= 2

        def start_all(b, carry):
            each_copy(b, lambda cp: cp.start())
            return carry

        def wait_all(b, carry):
            each_copy(b, lambda cp: cp.wait())
            return carry

        lax.fori_loop(0, fill_ref.shape[0], start_all, 0)
        lax.fori_loop(0, fill_ref.shape[0], wait_all, 0)

    def issue(g, carry):
        for u in range(DMA_UNROLL):
            t = pl.multiple_of(g * DMA_UNROLL, DMA_UNROLL) + u
            for kk in range(TOP_K):
                p = pos_ref[0, 0, kk * tm + t]
                pltpu.make_async_copy(rows_of(x_ref, t, 1), rows_of(xs_ref, p, 1), sem).start(priority=(u + kk) % 2)
        return carry

    lax.fori_loop(0, tm // DMA_UNROLL, issue, 0)
    for _ in range(TOP_K):
        pltpu.make_async_copy(x_ref, rows_of(xs_ref, 0, tm), sem).wait()


def _dispatch(blk_fill, pos_blocked, x1p, n_rows, chunks):
    n = x1p.shape[0] // chunks
    tm = DISPATCH_TM
    grid_spec = pltpu.PrefetchScalarGridSpec(
        num_scalar_prefetch=1,
        grid=(n // tm,),
        in_specs=[pl.BlockSpec((1, 1, TOP_K * tm), lambda i, fl: (i, 0, 0), memory_space=pltpu.SMEM),
                  pl.BlockSpec((tm * chunks, LANES), lambda i, fl: (i, 0))],
        out_specs=pl.BlockSpec(memory_space=pl.ANY),
        scratch_shapes=[pltpu.VMEM((EXPERT_ROWS * chunks, LANES), U32), pltpu.SemaphoreType.DMA(()),
                        pltpu.SemaphoreType.DMA(())],
    )
    return pl.pallas_call(
        functools.partial(_dispatch_kernel, chunks=chunks),
        grid_spec=grid_spec,
        out_shape=jax.ShapeDtypeStruct((n_rows * chunks, LANES), U32),
        compiler_params=_params(1),
    )(blk_fill, pos_blocked, x1p)


def _expert_up_kernel(blk_ref, col_ref, exp_ref, flag_ref, xs_ref, wg_ref, wu_ref, bg_ref, bu_ref, act_ref,
                      wgb_ref, wub_ref):
    del blk_ref, col_ref, exp_ref
    flag = flag_ref[pl.program_id(0)]

    @pl.when((flag & 2) != 0)
    def _():
        wgb_ref[...] = wg_ref[...].astype(BF16)
        wub_ref[...] = wu_ref[...].astype(BF16)

    @pl.when((flag & 1) != 0)
    def _():
        halves = [_unpack_halves(u) for u in _load_row_tiles(xs_ref, act_ref.shape[0])]
        xb = jnp.concatenate([lo.astype(BF16) for lo, _ in halves] + [hi.astype(BF16) for _, hi in halves], axis=1)
        for c in range(0, act_ref.shape[1], MXU_COLS):
            cs = slice(c, c + MXU_COLS)
            g = jnp.minimum(_dot(xb, wgb_ref[:, cs]) + bg_ref[:, cs], SWIGLU_LIMIT)
            u = jnp.clip(_dot(xb, wub_ref[:, cs]) + bu_ref[:, cs], -SWIGLU_LIMIT, SWIGLU_LIMIT)
            act_ref[:, cs] = ((u + 1.0) * (g * jax.nn.sigmoid(SWIGLU_ALPHA * g))).astype(act_ref.dtype)

    @pl.when((flag & 1) == 0)
    def _():
        act_ref[...] = jnp.zeros_like(act_ref)


def _expert_down_kernel(be_ref, nu_ref, first_ref, act_ref, wd_ref, bd_ref, ys_ref, wdb_ref):
    del be_ref
    i = pl.program_id(0)

    @pl.when(first_ref[i] != 0)
    def _():
        wdb_ref[...] = wd_ref[...].astype(BF16)

    @pl.when(i < nu_ref[0])
    def _():
        act = act_ref[...]
        half = wdb_ref.shape[1] // 2
        for c in range(0, half, MXU_COLS):
            lo = _dot(act, wdb_ref[:, c:c + MXU_COLS]) + bd_ref[:, c:c + MXU_COLS]
            hi = _dot(act, wdb_ref[:, half + c:half + c + MXU_COLS]) + bd_ref[:, half + c:half + c + MXU_COLS]
            _store_row_tiles(ys_ref, _pack_halves(jnp.concatenate([lo, hi], axis=1)), first=c // LANES)

    @pl.when(i >= nu_ref[0])
    def _():
        ys_ref[...] = jnp.zeros_like(ys_ref)


def _experts(layer, sched, xs, w_gu, b_gu, w_dn, b_dn):
    _, e, d, f2 = w_gu.shape
    chunks = d // 2 // LANES
    n_rows = xs.shape[0] // chunks
    f = f2 // 2
    tb, fh = EXPERT_ROWS, EXPERT_UP_COLS
    nh = f // fh
    nb = n_rows // tb
    up_spec = pltpu.PrefetchScalarGridSpec(
        num_scalar_prefetch=4,
        grid=(nh * nb,),
        in_specs=[pl.BlockSpec((tb * chunks, LANES), lambda t, bl, co, ex, fl: (bl[t], 0)),
                  pl.BlockSpec((None, None, d, fh), lambda t, bl, co, ex, fl: (layer, ex[t], 0, co[t])),
                  pl.BlockSpec((None, None, d, fh), lambda t, bl, co, ex, fl: (layer, ex[t], 0, nh + co[t])),
                  pl.BlockSpec((None, None, 1, fh), lambda t, bl, co, ex, fl: (layer, ex[t], 0, co[t])),
                  pl.BlockSpec((None, None, 1, fh), lambda t, bl, co, ex, fl: (layer, ex[t], 0, nh + co[t]))],
        out_specs=pl.BlockSpec((tb, fh), lambda t, bl, co, ex, fl: (bl[t], co[t])),
        scratch_shapes=[pltpu.VMEM((d, fh), BF16), pltpu.VMEM((d, fh), BF16)],
    )
    b_gu4 = b_gu.reshape(b_gu.shape[0], e, 1, f2)
    act = pl.pallas_call(
        _expert_up_kernel,
        grid_spec=up_spec,
        out_shape=jax.ShapeDtypeStruct((n_rows, f), BF16),
        compiler_params=_params(1),
    )(sched["it_blk"], sched["it_col"], sched["it_exp"], sched["it_flag"], xs, w_gu, w_gu, b_gu4, b_gu4)

    def blk(i, nu):
        return jnp.minimum(i, nu[0] - 1)

    down_spec = pltpu.PrefetchScalarGridSpec(
        num_scalar_prefetch=3,
        grid=(nb,),
        in_specs=[pl.BlockSpec((tb, f), lambda i, be, nu, fi: (blk(i, nu), 0)),
                  pl.BlockSpec((None, None, f, d), lambda i, be, nu, fi: (layer, be[i], 0, 0)),
                  pl.BlockSpec((None, None, 1, d), lambda i, be, nu, fi: (layer, be[i], 0, 0))],
        out_specs=pl.BlockSpec((tb * chunks, LANES), lambda i, be, nu, fi: (i, 0)),
        scratch_shapes=[pltpu.VMEM((f, d), BF16)],
    )
    return pl.pallas_call(
        _expert_down_kernel,
        grid_spec=down_spec,
        out_shape=jax.ShapeDtypeStruct((n_rows * chunks, LANES), U32),
        compiler_params=_params(1),
    )(sched["blk_e"], sched["n_used"], sched["blk_first"], act, w_dn, b_dn.reshape(b_dn.shape[0], e, 1, d))


def _expert_schedule(total, n_rows, e, f):
    tb = EXPERT_ROWS
    nh = f // EXPERT_UP_COLS
    nb = n_rows // tb
    padded = (total + tb - 1) // tb * tb
    pad_end = jnp.cumsum(padded)
    pad_start = pad_end - padded
    n_used = pad_end[-1] // tb
    blocks = jnp.arange(nb, dtype=I32)
    blk_e = jnp.minimum(jnp.sum(pad_end[None, :] <= (blocks * tb)[:, None], axis=1), e - 1).astype(I32)
    blk_e = jnp.where(blocks < n_used, blk_e, blk_e[n_used - 1])
    blk_first = jnp.concatenate([jnp.ones((1,), I32), (blk_e[1:] != blk_e[:-1]).astype(I32)])
    grp_last = blocks == pad_end[blk_e] // tb - 1
    blk_fill = jnp.where(blocks < n_used, jnp.where(grp_last, (padded - total)[blk_e], 0), tb).astype(I32)
    items = jnp.arange(nh * nb, dtype=I32)
    grp_end = nh * (pad_end // tb)
    it_e = jnp.minimum(jnp.sum(grp_end[None, :] <= items[:, None], axis=1), e - 1).astype(I32)
    grp_blocks = jnp.maximum((padded // tb)[it_e], 1)
    r = items - nh * (pad_start // tb)[it_e]
    valid = items < nh * n_used
    spare = items - nh * n_used
    it_blk = jnp.where(valid, (pad_start // tb)[it_e] + r % grp_blocks, n_used + spare // nh)
    it_col = jnp.where(valid, r // grp_blocks, spare % nh)
    it_exp = jnp.where(valid, it_e, blk_e[n_used - 1])
    it_flag = jnp.where(valid, 1 + 2 * (r % grp_blocks == 0).astype(I32), 0)
    it_flag = it_flag.at[0].set(it_flag[0] | 2)
    return dict(pad_start=pad_start, blk_fill=blk_fill, blk_e=blk_e, n_used=n_used.astype(I32).reshape(1),
                blk_first=blk_first, it_blk=it_blk.astype(I32), it_col=it_col.astype(I32),
                it_exp=it_exp.astype(I32), it_flag=it_flag.astype(I32))


def _combine_kernel(pos_ref, nxt_ref, ys_ref, x1_ref, gcol_ref, p_ref, g_ref, b_ref, wpg_ref, wpp_ref, out_ref,
                    buf_ref, sem, *, alpha):
    tm = x1_ref.shape[0]
    chunks = buf_ref.shape[2] // tm
    i = pl.program_id(0)
    slot = i % 2

    def gather(rows_ref, dst):
        def issue(g, carry):
            for u in range(DMA_UNROLL):
                t = pl.multiple_of(g * DMA_UNROLL, DMA_UNROLL) + u
                for kk in range(TOP_K):
                    r = rows_ref[0, 0, kk * tm + t]
                    pltpu.make_async_copy(ys_ref.at[pl.ds(pl.multiple_of(r * chunks, chunks), chunks)],
                                          buf_ref.at[dst, kk, pl.ds(pl.multiple_of(t * chunks, chunks), chunks)],
                                          sem.at[dst]).start(priority=(u + kk) % 2)
            return carry

        lax.fori_loop(0, tm // DMA_UNROLL, issue, 0)

    @pl.when(i == 0)
    def _():
        gather(pos_ref, 0)

    @pl.when(i + 1 < pl.num_programs(0))
    def _():
        gather(nxt_ref, 1 - slot)

    for kk in range(TOP_K):
        pltpu.make_async_copy(ys_ref.at[pl.ds(0, tm * chunks)], buf_ref.at[slot, kk], sem.at[slot]).wait()

    gates = gcol_ref[...]
    lo = hi = None
    for kk in range(TOP_K):
        gk = gates[:, kk:kk + 1]
        halves = [_unpack_halves(u) for u in _load_row_tiles(buf_ref, tm, (slot, kk))]
        lo = [gk * l for l, _ in halves] if lo is None else [a + gk * l for a, (l, _) in zip(lo, halves)]
        hi = [gk * h for _, h in halves] if hi is None else [a + gk * h for a, (_, h) in zip(hi, halves)]
    ffn = jnp.concatenate(lo + hi, axis=1)
    x2 = _layer_norm(alpha * x1_ref[...] + ffn, g_ref[...], b_ref[...])
    gate = jax.nn.sigmoid(_dot(x2.astype(BF16), wpg_ref[...]))
    out_ref[...] = x2 + gate * _dot(p_ref[...].astype(BF16), wpp_ref[...])


def _combine(pos_blocked, ys, x1, gcol, p, ln_g, ln_b, w_pg, w_pp, alpha):
    n, d = x1.shape
    tm = COMBINE_TM
    pd = p.shape[1]
    full = lambda i: (0, 0)
    once = pl.Buffered(1)
    last = n // tm - 1
    return pl.pallas_call(
        functools.partial(_combine_kernel, alpha=alpha),
        grid=(n // tm,),
        in_specs=[pl.BlockSpec((1, 1, TOP_K * tm), lambda i: (i, 0, 0), memory_space=pltpu.SMEM),
                  pl.BlockSpec((1, 1, TOP_K * tm), lambda i: (jnp.minimum(i + 1, last), 0, 0),
                               memory_space=pltpu.SMEM),
                  pl.BlockSpec(memory_space=pl.ANY),
                  pl.BlockSpec((tm, d), lambda i: (i, 0)),
                  pl.BlockSpec((tm, LANES), lambda i: (i, 0)),
                  pl.BlockSpec((tm, pd), lambda i: (i, 0)),
                  pl.BlockSpec((1, d), full),
                  pl.BlockSpec((1, d), full),
                  pl.BlockSpec((d, d), full, pipeline_mode=once),
                  pl.BlockSpec((pd, d), full, pipeline_mode=once)],
        out_specs=pl.BlockSpec((tm, d), lambda i: (i, 0)),
        out_shape=jax.ShapeDtypeStruct((n, d), F32),
        scratch_shapes=[pltpu.VMEM((2, TOP_K, tm * (d // 2 // LANES), LANES), U32), pltpu.SemaphoreType.DMA((2,))],
        compiler_params=_params(1),
    )(pos_blocked, pos_blocked, ys, x1, gcol, p, ln_g.reshape(1, d), ln_b.reshape(1, d), w_pg, w_pp)


def _moe_tail(layer, x1, x1p, topi, gcol, cnt, p, ln_g, ln_b, w_gu, b_gu, w_dn, b_dn, w_pg, w_pp, alpha):
    n, d = x1.shape
    e = w_gu.shape[1]
    e_pad = cnt.shape[1]
    tb = EXPERT_ROWS
    c = cnt[:, :, 0].astype(I32)
    c = c.reshape(n // POS_TM, POS_TM // POST_TM, e_pad).sum(axis=1)
    n_rows = -(-n * TOP_K // tb) * tb + e * tb
    sched = _expert_schedule(c.sum(axis=0), n_rows, e, w_gu.shape[3] // 2)
    blk_off = (sched["pad_start"][None, :] + jnp.cumsum(c, axis=0) - c).astype(F32)[:, :, None]

    pos = _positions(topi, blk_off, e_pad)
    xs = _dispatch(sched["blk_fill"], _blocked_pos(pos, DISPATCH_TM), x1p, n_rows, d // 2 // LANES)
    ys = _experts(layer, sched, xs, w_gu, b_gu, w_dn, b_dn)
    return _combine(_blocked_pos(pos, COMBINE_TM), ys, x1, gcol, p, ln_g, ln_b, w_pg, w_pp, alpha)


def kernel(x, p, ln_g, ln_b, w_in_a, b_gate_a, norm_a, w_out_a, w_in_b, conv_b, w_out_b, w_router, b_router,
           w_gu, b_gu, w_dn, b_dn, w_ple_gate, w_ple_proj):
    bsz, seq, d = x.shape
    depth = ln_g.shape[0]
    n = bsz * seq
    alpha = (2 * depth) ** 0.25
    heads = b_gate_a.shape[-1] // 2
    dv = d // heads
    dk = dv // 2
    qkvo = 2 * heads * dk + 2 * heads * dv
    xf = x.reshape(n, d)
    for i in range(depth):
        j = i // 2
        if i % 2 == 0:
            col, row = _mlstm_gates(xf, w_in_a[j][:, qkvo:], b_gate_a[j], heads)
            proj = _proj(xf, w_in_a[j][:, :qkvo].astype(BF16), PROJ_TM, PROJ_TN_A)
            a = _mlstm(proj, col, row, norm_a[j], bsz, seq, heads, dk, dv)
            w_out = w_out_a[j]
        else:
            a = _conv_front(xf, w_in_b[j].astype(BF16), conv_b[j], seq)
            w_out = w_out_b[j]
        x1, x1p, topi, gcol, cnt = _post_mixer(a, w_out.astype(BF16), xf, ln_g[i, 0], ln_b[i, 0],
                                               w_router[i], b_router[i], alpha)
        xf = _moe_tail(i, x1, x1p, topi, gcol, cnt, p[i].reshape(n, -1), ln_g[i, 1], ln_b[i, 1],
                       w_gu, b_gu, w_dn, b_dn,
                       w_ple_gate[i].astype(BF16), w_ple_proj[i].astype(BF16), alpha)
    return xf.reshape(bsz, seq, d)
```

```python
import functools

import jax
import jax.numpy as jnp
from jax import lax
from jax.experimental import pallas as pl
from jax.experimental.pallas import tpu as pltpu

F32 = jnp.float32
BF16 = jnp.bfloat16
U32 = jnp.uint32
I32 = jnp.int32

GATE_SOFTCAP = 15.0
CONV_WIDTH = 3
TOP_K = 4
SWIGLU_ALPHA = 1.702
SWIGLU_LIMIT = 7.0
LN_EPS = 1e-5
RMS_EPS = 1e-6

LANES = 128
SUBLANES = 8
MXU_COLS = 256
VMEM_LIMIT_BYTES = 60000 * 1024

MLSTM_ROWS = 256
PROJ_TM = 1024
PROJ_TN_A = 1024
PROJ_TN_B = 512
POST_TM = 512
POS_TM = 512
DISPATCH_TM = 1024
EXPERT_ROWS = 512
EXPERT_UP_COLS = 1024
COMBINE_TM = 256
DMA_UNROLL = 8


def _dot(a, b):
    return jnp.dot(a, b, preferred_element_type=F32)


def _dot_nt(a, b):
    return lax.dot_general(a, b, (((1,), (1,)), ((), ())), preferred_element_type=F32)


def _dot_tn(a, b):
    return lax.dot_general(a, b, (((0,), (0,)), ((), ())), preferred_element_type=F32)


def _split2(x):
    hi = x.astype(BF16)
    lo = (x - hi.astype(F32)).astype(BF16)
    return hi, lo


def _split3(x):
    hi = x.astype(BF16)
    r = x - hi.astype(F32)
    mid = r.astype(BF16)
    lo = (r - mid.astype(F32)).astype(BF16)
    return hi, mid, lo


def _pack_halves(y):
    half = y.shape[1] // 2
    lo = lax.bitcast_convert_type(y[:, :half].astype(BF16).astype(F32), U32) >> 16
    hi = lax.bitcast_convert_type(y[:, half:].astype(BF16).astype(F32), U32) & jnp.uint32(0xFFFF0000)
    return lo | hi


def _unpack_halves(u):
    lo = lax.bitcast_convert_type(u << 16, F32)
    hi = lax.bitcast_convert_type(u & jnp.uint32(0xFFFF0000), F32)
    return lo, hi


def _store_row_tiles(ref, words, first=0):
    r = words.shape[0]
    c = ref.shape[0] // r
    for j in range(words.shape[1] // LANES):
        ref[pl.ds(first + j, r, stride=c), :] = words[:, j * LANES:(j + 1) * LANES]


def _load_row_tiles(ref, r, index=()):
    c = ref.shape[-2] // r
    return [ref[index + (pl.ds(j, r, stride=c), slice(None))] for j in range(c)]


def _layer_norm(y, g, b):
    mu = jnp.mean(y, axis=-1, keepdims=True)
    yc = y - mu
    var = jnp.mean(yc * yc, axis=-1, keepdims=True)
    return yc * lax.rsqrt(var + LN_EPS) * g + b


def _params(n_axes):
    return pltpu.CompilerParams(dimension_semantics=("arbitrary",) * n_axes,
                                vmem_limit_bytes=VMEM_LIMIT_BYTES)


def _gate_kernel(x_ref, w_ref, b_ref, col_ref, row_ref, *, heads):
    rows = x_ref.shape[0]
    xh, xl = _split2(x_ref[...])
    wh, wl = _split2(w_ref[...])
    g = _dot(xh, wh) + _dot(xh, wl) + _dot(xl, wh) + b_ref[...]
    g = GATE_SOFTCAP * jnp.tanh(g / GATE_SOFTCAP)
    lf = -jnp.log1p(jnp.exp(-g))
    r_i = lax.broadcasted_iota(I32, (rows, rows), 0)
    c_i = lax.broadcasted_iota(I32, (rows, rows), 1)
    tril = (c_i <= r_i).astype(BF16)
    p0, p1, p2 = _split3(lf)
    bsum = _dot(tril, p0) + _dot(tril, p1) + _dot(tril, p2)
    lane = lax.broadcasted_iota(I32, g.shape, 1)
    col = jnp.where(lane < heads, g, jnp.where(lane < 2 * heads, bsum, 0.0))
    col_ref[...] = col
    row_ref[...] = col.T[: row_ref.shape[0]]


def _mlstm_gates(xf, w_g, b_g, heads):
    n, d = xf.shape
    rows = MLSTM_ROWS
    r_pad = -(-2 * heads // SUBLANES) * SUBLANES
    wg = jnp.zeros((d, LANES), F32).at[:, : 2 * heads].set(w_g)
    bg = jnp.zeros((1, LANES), F32).at[0, : 2 * heads].set(b_g)
    return pl.pallas_call(
        functools.partial(_gate_kernel, heads=heads),
        grid=(n // rows,),
        in_specs=[pl.BlockSpec((rows, d), lambda i: (i, 0)),
                  pl.BlockSpec((d, LANES), lambda i: (0, 0)),
                  pl.BlockSpec((1, LANES), lambda i: (0, 0))],
        out_specs=[pl.BlockSpec((rows, LANES), lambda i: (i, 0)),
                   pl.BlockSpec((r_pad, rows), lambda i: (0, i))],
        out_shape=[jax.ShapeDtypeStruct((n, LANES), F32),
                   jax.ShapeDtypeStruct((r_pad, n), F32)],
        compiler_params=_params(1),
    )(xf, wg, bg)


def _proj_kernel(x_ref, w_ref, o_ref, xb_ref):
    @pl.when(pl.program_id(1) == 0)
    def _():
        xb_ref[...] = x_ref[...].astype(BF16)

    o_ref[...] = _dot(xb_ref[...], w_ref[...]).astype(o_ref.dtype)


def _proj(xf, w, tm, tn):
    n, d = xf.shape
    m = w.shape[1]
    while m % tn:
        tn //= 2
    return pl.pallas_call(
        _proj_kernel,
        grid=(n // tm, m // tn),
        in_specs=[pl.BlockSpec((tm, d), lambda i, j: (i, 0)),
                  pl.BlockSpec((d, tn), lambda i, j: (0, j))],
        out_specs=pl.BlockSpec((tm, tn), lambda i, j: (i, j)),
        out_shape=jax.ShapeDtypeStruct((n, m), BF16),
        scratch_shapes=[pltpu.VMEM((tm, d), BF16)],
        compiler_params=_params(2),
    )(xf, w)


def _mlstm_kernel(q_ref, k_ref, v_ref, o_ref, col_ref, row_ref, norm_ref, out_ref, c_ref, n_ref, *, heads, dk, dv):
    rows = q_ref.shape[0]
    scale = dk ** -0.5

    @pl.when(pl.program_id(1) == 0)
    def _():
        c_ref[...] = jnp.zeros_like(c_ref)
        n_ref[...] = jnp.zeros_like(n_ref)

    col = col_ref[...]
    row = row_ref[...]
    t_i = lax.broadcasted_iota(I32, (rows, rows), 0)
    s_i = lax.broadcasted_iota(I32, (rows, rows), 1)
    causal = s_i <= t_i
    for h in range(heads):
        q = q_ref[:, h * dk:(h + 1) * dk]
        k = k_ref[:, h * dk:(h + 1) * dk]
        v = v_ref[:, h * dv:(h + 1) * dv]
        li_c = col[:, h:h + 1]
        b_c = col[:, heads + h:heads + h + 1]
        li_r = row[h:h + 1, :]
        b_r = row[heads + h:heads + h + 1, :]
        b_last = b_c[rows - 1:rows, :]
        decay_w = jnp.where(causal, jnp.exp(b_c - b_r + li_r), 0.0)
        sw = _dot_nt(q, k) * scale * decay_w
        eb = jnp.exp(b_c) * scale
        c_prev = c_ref[h]
        n_prev = n_ref[h]
        num = _dot(sw.astype(BF16), v) + eb * _dot(q, c_prev.astype(BF16))
        qn = jnp.sum(q.astype(F32) * n_prev, axis=-1, keepdims=True)
        den = jnp.sum(sw, axis=-1, keepdims=True) + eb * qn
        hh = num / jnp.maximum(jnp.abs(den), 1.0)
        wk = jnp.exp(b_last - b_c + li_c)
        carry = jnp.exp(b_last)
        kf = k.astype(F32) * wk
        c_ref[h] = carry * c_prev + _dot_tn(kf.astype(BF16), v)
        n_ref[h] = carry * n_prev + jnp.sum(kf, axis=0, keepdims=True)
        hh = hh * lax.rsqrt(jnp.mean(hh * hh, axis=-1, keepdims=True) + RMS_EPS)
        hh = hh * norm_ref[:, h * dv:(h + 1) * dv]
        og = o_ref[:, h * dv:(h + 1) * dv].astype(F32)
        out_ref[:, h * dv:(h + 1) * dv] = (hh * jax.nn.sigmoid(og)).astype(out_ref.dtype)


def _mlstm(proj, col, row, norm, bsz, seq, heads, dk, dv):
    rows = MLSTM_ROWS
    nc = seq // rows
    qk_w, v_w = heads * dk, heads * dv
    assert v_w == 2 * qk_w
    r_pad = row.shape[0]
    tok = lambda b, c: b * nc + c
    return pl.pallas_call(
        functools.partial(_mlstm_kernel, heads=heads, dk=dk, dv=dv),
        grid=(bsz, nc),
        in_specs=[pl.BlockSpec((rows, qk_w), lambda b, c: (tok(b, c), 0)),
                  pl.BlockSpec((rows, qk_w), lambda b, c: (tok(b, c), 1)),
                  pl.BlockSpec((rows, v_w), lambda b, c: (tok(b, c), 1)),
                  pl.BlockSpec((rows, v_w), lambda b, c: (tok(b, c), 2)),
                  pl.BlockSpec((rows, LANES), lambda b, c: (tok(b, c), 0)),
                  pl.BlockSpec((r_pad, rows), lambda b, c: (0, tok(b, c))),
                  pl.BlockSpec((1, v_w), lambda b, c: (0, 0))],
        out_specs=pl.BlockSpec((rows, v_w), lambda b, c: (tok(b, c), 0)),
        out_shape=jax.ShapeDtypeStruct((bsz * seq, v_w), BF16),
        scratch_shapes=[pltpu.VMEM((heads, dk, dv), F32), pltpu.VMEM((heads, 1, dk), F32)],
        compiler_params=_params(2),
    )(proj, proj, proj, proj, col, row, norm.reshape(1, v_w))


def _conv_kernel(x_ref, wb_ref, wc_ref, wu_ref, cw_ref, o_ref, xb_ref, carry_ref, *, steps_per_seq):
    i, j = pl.program_id(0), pl.program_id(1)
    tm = x_ref.shape[0]

    @pl.when(j == 0)
    def _():
        xb_ref[...] = x_ref[...].astype(BF16)

    xb = xb_ref[...]
    z = _dot(xb, wc_ref[...]) * _dot(xb, wu_ref[...])
    @pl.when(i % steps_per_seq == 0)
    def _():
        carry_ref[j] = jnp.zeros(carry_ref.shape[1:], F32)

    prev = carry_ref[j]
    carry_ref[j] = z[tm - SUBLANES:, :]
    r_i = lax.broadcasted_iota(I32, z.shape, 0)
    z1 = jnp.where(r_i == 0, prev[SUBLANES - 1:SUBLANES, :], pltpu.roll(z, 1, 0))
    z2 = jnp.where(r_i == 0, prev[SUBLANES - 2:SUBLANES - 1, :],
                   jnp.where(r_i == 1, prev[SUBLANES - 1:SUBLANES, :], pltpu.roll(z, 2, 0)))
    cw = cw_ref[...]
    zc = cw[0:1, :] * z2 + cw[1:2, :] * z1 + cw[2:3, :] * z
    o_ref[...] = (_dot(xb, wb_ref[...]) * zc).astype(o_ref.dtype)


def _conv_front(xf, w_in, conv_w, seq):
    n, d = xf.shape
    tm, tn = PROJ_TM, PROJ_TN_B
    nj = d // tn
    cw = jnp.zeros((SUBLANES, d), F32).at[:CONV_WIDTH].set(conv_w)
    return pl.pallas_call(
        functools.partial(_conv_kernel, steps_per_seq=seq // tm),
        grid=(n // tm, nj),
        in_specs=[pl.BlockSpec((tm, d), lambda i, j: (i, 0)),
                  pl.BlockSpec((d, tn), lambda i, j: (0, j)),
                  pl.BlockSpec((d, tn), lambda i, j: (0, nj + j)),
                  pl.BlockSpec((d, tn), lambda i, j: (0, 2 * nj + j)),
                  pl.BlockSpec((SUBLANES, tn), lambda i, j: (0, j))],
        out_specs=pl.BlockSpec((tm, tn), lambda i, j: (i, j)),
        out_shape=jax.ShapeDtypeStruct((n, d), BF16),
        scratch_shapes=[pltpu.VMEM((tm, d), BF16), pltpu.VMEM((nj, SUBLANES, tn), F32)],
        compiler_params=_params(2),
    )(xf, w_in, w_in, w_in, cw)


def _post_kernel(a_ref, w_ref, x_ref, g_ref, b_ref, wr_ref, br_ref,
                 x1_ref, x1p_ref, topi_ref, gcol_ref, cnt_ref, *, alpha, e_pad):
    tm = a_ref.shape[0]
    y = alpha * x_ref[...] + _dot(a_ref[...], w_ref[...])
    x1 = _layer_norm(y, g_ref[...], b_ref[...])
    x1_ref[...] = x1
    _store_row_tiles(x1p_ref, _pack_halves(x1))
    xh, xl = _split2(x1)
    w2 = wr_ref[...]
    l2 = _dot(xh, w2)
    logits = l2[:, :LANES] + l2[:, LANES:] + _dot(xl, w2[:, :LANES]) + br_ref[...]
    lt = logits.T[:e_pad]
    e_i = lax.broadcasted_iota(I32, lt.shape, 0)
    cur = lt
    ids, vals = [], []
    for _ in range(TOP_K):
        m = jnp.max(cur, axis=0, keepdims=True)
        idx = jnp.min(jnp.where(cur == m, e_i, e_pad), axis=0, keepdims=True)
        cur = jnp.where(e_i == idx, -jnp.inf, cur)
        ids.append(idx)
        vals.append(m)
    ex = [jnp.exp(v - vals[0]) for v in vals]
    den = ex[0] + ex[1] + ex[2] + ex[3]
    k8 = lax.broadcasted_iota(I32, (SUBLANES, tm), 0)
    k128 = lax.broadcasted_iota(I32, (LANES, tm), 0)
    topi = jnp.zeros((SUBLANES, tm), I32)
    gmat = jnp.zeros((LANES, tm), F32)
    for kk in range(TOP_K):
        topi = jnp.where(k8 == kk, ids[kk], topi)
        gmat = jnp.where(k128 == kk, ex[kk] / den, gmat)
    topi_ref[...] = topi
    gcol_ref[...] = gmat.T
    sel = (cur == -jnp.inf).astype(F32)
    cnt_ref[0] = jnp.broadcast_to(jnp.sum(sel, axis=1, keepdims=True), (e_pad, LANES))


def _post_mixer(a, w_out, xf, ln_g, ln_b, w_router, b_router, alpha):
    n, d = xf.shape
    tm = POST_TM
    e = w_router.shape[1]
    assert TOP_K <= e <= LANES
    e_pad = -(-e // SUBLANES) * SUBLANES
    wr = jnp.zeros((d, LANES), F32).at[:, :e].set(w_router)
    wr_hi = wr.astype(BF16)
    wr_lo = (wr - wr_hi.astype(F32)).astype(BF16)
    wr2 = jnp.concatenate([wr_hi, wr_lo], axis=1)
    br = jnp.full((1, LANES), -1e30, F32).at[0, :e].set(b_router)
    full = lambda i: (0, 0)
    once = pl.Buffered(1)
    chunks = d // 2 // LANES
    return pl.pallas_call(
        functools.partial(_post_kernel, alpha=alpha, e_pad=e_pad),
        grid=(n // tm,),
        in_specs=[pl.BlockSpec((tm, d), lambda i: (i, 0)),
                  pl.BlockSpec((d, d), full, pipeline_mode=once),
                  pl.BlockSpec((tm, d), lambda i: (i, 0)),
                  pl.BlockSpec((1, d), full),
                  pl.BlockSpec((1, d), full),
                  pl.BlockSpec((d, 2 * LANES), full, pipeline_mode=once),
                  pl.BlockSpec((1, LANES), full)],
        out_specs=[pl.BlockSpec((tm, d), lambda i: (i, 0)),
                   pl.BlockSpec((tm * chunks, LANES), lambda i: (i, 0)),
                   pl.BlockSpec((SUBLANES, tm), lambda i: (0, i)),
                   pl.BlockSpec((tm, LANES), lambda i: (i, 0)),
                   pl.BlockSpec((1, e_pad, LANES), lambda i: (i, 0, 0))],
        out_shape=[jax.ShapeDtypeStruct((n, d), F32),
                   jax.ShapeDtypeStruct((n * chunks, LANES), U32),
                   jax.ShapeDtypeStruct((SUBLANES, n), I32),
                   jax.ShapeDtypeStruct((n, LANES), F32),
                   jax.ShapeDtypeStruct((n // tm, e_pad, LANES), F32)],
        compiler_params=_params(1),
    )(a, w_out, xf, ln_g.reshape(1, d), ln_b.reshape(1, d), wr2, br)


def _pos_kernel(topi_ref, off_ref, pos_ref, *, e_pad):
    tm = topi_ref.shape[1]
    ti = topi_ref[...]
    e_i = lax.broadcasted_iota(I32, (e_pad, tm), 0)
    hits = [e_i == ti[kk:kk + 1, :] for kk in range(TOP_K)]
    sel = hits[0] | hits[1] | hits[2] | hits[3]
    s_i = lax.broadcasted_iota(I32, (tm, tm), 0)
    t_i = lax.broadcasted_iota(I32, (tm, tm), 1)
    before = (s_i < t_i).astype(BF16)
    cum = _dot(sel.astype(BF16), before) + off_ref[0]
    k8 = lax.broadcasted_iota(I32, (SUBLANES, tm), 0)
    pos = jnp.zeros((SUBLANES, tm), F32)
    for kk in range(TOP_K):
        pk = jnp.sum(jnp.where(hits[kk], cum, 0.0), axis=0, keepdims=True)
        pos = jnp.where(k8 == kk, pk, pos)
    pos_ref[...] = pos.astype(I32)


def _positions(topi, blk_off, e_pad):
    n = topi.shape[1]
    tm = POS_TM
    return pl.pallas_call(
        functools.partial(_pos_kernel, e_pad=e_pad),
        grid=(n // tm,),
        in_specs=[pl.BlockSpec((SUBLANES, tm), lambda i: (0, i)),
                  pl.BlockSpec((1, e_pad, 1), lambda i: (i, 0, 0))],
        out_specs=pl.BlockSpec((SUBLANES, tm), lambda i: (0, i)),
        out_shape=jax.ShapeDtypeStruct((SUBLANES, n), I32),
        compiler_params=_params(1),
    )(topi, blk_off)


def _blocked_pos(pos, tm):
    n = pos.shape[1]
    p = pos[:TOP_K].reshape(TOP_K, n // tm, tm)
    return jnp.transpose(p, (1, 0, 2)).reshape(n // tm, 1, TOP_K * tm)


def _dispatch_kernel(fill_ref, pos_ref, x_ref, xs_ref, zero_ref, sem, fill_sem, *, chunks):
    tm = x_ref.shape[0] // chunks
    tb = zero_ref.shape[0] // chunks

    def rows_of(ref, first, count):
        return ref.at[pl.ds(pl.multiple_of(first * chunks, chunks), count * chunks)]

    @pl.when(pl.program_id(0) == 0)
    def _():
        zero_ref[...] = jnp.zeros_like(zero_ref)

        def each_copy(b, action):
            cnt = fill_ref[b]
            row = (b + 1) * tb - cnt
            bit = tb
            while bit:
                @pl.when((cnt & bit) != 0)
                def _(row=row, bit=bit):
                    action(pltpu.make_async_copy(rows_of(zero_ref, 0, bit), rows_of(xs_ref, row, bit), fill_sem))

                row = row + (cnt & bit)
                bit //= 2

        def start_all(b, carry):
            each_copy(b, lambda cp: cp.start())
            return carry

        def wait_all(b, carry):
            each_copy(b, lambda cp: cp.wait())
            return carry

        lax.fori_loop(0, fill_ref.shape[0], start_all, 0)
        lax.fori_loop(0, fill_ref.shape[0], wait_all, 0)

    def issue(g, carry):
        for u in range(DMA_UNROLL):
            t = pl.multiple_of(g * DMA_UNROLL, DMA_UNROLL) + u
            for kk in range(TOP_K):
                p = pos_ref[0, 0, kk * tm + t]
                pltpu.make_async_copy(rows_of(x_ref, t, 1), rows_of(xs_ref, p, 1), sem).start(priority=(u + kk) % 2)
        return carry

    lax.fori_loop(0, tm // DMA_UNROLL, issue, 0)
    for _ in range(TOP_K):
        pltpu.make_async_copy(x_ref, rows_of(xs_ref, 0, tm), sem).wait()


def _dispatch(blk_fill, pos_blocked, x1p, n_rows, chunks):
    n = x1p.shape[0] // chunks
    tm = DISPATCH_TM
    grid_spec = pltpu.PrefetchScalarGridSpec(
        num_scalar_prefetch=1,
        grid=(n // tm,),
        in_specs=[pl.BlockSpec((1, 1, TOP_K * tm), lambda i, fl: (i, 0, 0), memory_space=pltpu.SMEM),
                  pl.BlockSpec((tm * chunks, LANES), lambda i, fl: (i, 0))],
        out_specs=pl.BlockSpec(memory_space=pl.ANY),
        scratch_shapes=[pltpu.VMEM((EXPERT_ROWS * chunks, LANES), U32), pltpu.SemaphoreType.DMA(()),
                        pltpu.SemaphoreType.DMA(())],
    )
    return pl.pallas_call(
        functools.partial(_dispatch_kernel, chunks=chunks),
        grid_spec=grid_spec,
        out_shape=jax.ShapeDtypeStruct((n_rows * chunks, LANES), U32),
        compiler_params=_params(1),
    )(blk_fill, pos_blocked, x1p)


def _expert_up_kernel(blk_ref, col_ref, exp_ref, flag_ref, xs_ref, wg_ref, wu_ref, bg_ref, bu_ref, act_ref,
                      wgb_ref, wub_ref):
    del blk_ref, col_ref, exp_ref
    flag = flag_ref[pl.program_id(0)]

    @pl.when((flag & 2) != 0)
    def _():
        wgb_ref[...] = wg_ref[...].astype(BF16)
        wub_ref[...] = wu_ref[...].astype(BF16)

    @pl.when((flag & 1) != 0)
    def _():
        halves = [_unpack_halves(u) for u in _load_row_tiles(xs_ref, act_ref.shape[0])]
        xb = jnp.concatenate([lo.astype(BF16) for lo, _ in halves] + [hi.astype(BF16) for _, hi in halves], axis=1)
        for c in range(0, act_ref.shape[1], MXU_COLS):
            cs = slice(c, c + MXU_COLS)
            g = jnp.minimum(_dot(xb, wgb_ref[:, cs]) + bg_ref[:, cs], SWIGLU_LIMIT)
            u = jnp.clip(_dot(xb, wub_ref[:, cs]) + bu_ref[:, cs], -SWIGLU_LIMIT, SWIGLU_LIMIT)
            act_ref[:, cs] = ((u + 1.0) * (g * jax.nn.sigmoid(SWIGLU_ALPHA * g))).astype(act_ref.dtype)

    @pl.when((flag & 1) == 0)
    def _():
        act_ref[...] = jnp.zeros_like(act_ref)


def _expert_down_kernel(be_ref, nu_ref, first_ref, act_ref, wd_ref, bd_ref, ys_ref, wdb_ref):
    del be_ref
    i = pl.program_id(0)

    @pl.when(first_ref[i] != 0)
    def _():
        wdb_ref[...] = wd_ref[...].astype(BF16)

    @pl.when(i < nu_ref[0])
    def _():
        act = act_ref[...]
        half = wdb_ref.shape[1] // 2
        for c in range(0, half, MXU_COLS):
            lo = _dot(act, wdb_ref[:, c:c + MXU_COLS]) + bd_ref[:, c:c + MXU_COLS]
            hi = _dot(act, wdb_ref[:, half + c:half + c + MXU_COLS]) + bd_ref[:, half + c:half + c + MXU_COLS]
            _store_row_tiles(ys_ref, _pack_halves(jnp.concatenate([lo, hi], axis=1)), first=c // LANES)

    @pl.when(i >= nu_ref[0])
    def _():
        ys_ref[...] = jnp.zeros_like(ys_ref)


def _experts(layer, sched, xs, w_gu, b_gu, w_dn, b_dn):
    _, e, d, f2 = w_gu.shape
    chunks = d // 2 // LANES
    n_rows = xs.shape[0] // chunks
    f = f2 // 2
    tb, fh = EXPERT_ROWS, EXPERT_UP_COLS
    nh = f // fh
    nb = n_rows // tb
    up_spec = pltpu.PrefetchScalarGridSpec(
        num_scalar_prefetch=4,
        grid=(nh * nb,),
        in_specs=[pl.BlockSpec((tb * chunks, LANES), lambda t, bl, co, ex, fl: (bl[t], 0)),
                  pl.BlockSpec((None, None, d, fh), lambda t, bl, co, ex, fl: (layer, ex[t], 0, co[t])),
                  pl.BlockSpec((None, None, d, fh), lambda t, bl, co, ex, fl: (layer, ex[t], 0, nh + co[t])),
                  pl.BlockSpec((None, None, 1, fh), lambda t, bl, co, ex, fl: (layer, ex[t], 0, co[t])),
                  pl.BlockSpec((None, None, 1, fh), lambda t, bl, co, ex, fl: (layer, ex[t], 0, nh + co[t]))],
        out_specs=pl.BlockSpec((tb, fh), lambda t, bl, co, ex, fl: (bl[t], co[t])),
        scratch_shapes=[pltpu.VMEM((d, fh), BF16), pltpu.VMEM((d, fh), BF16)],
    )
    b_gu4 = b_gu.reshape(b_gu.shape[0], e, 1, f2)
    act = pl.pallas_call(
        _expert_up_kernel,
        grid_spec=up_spec,
        out_shape=jax.ShapeDtypeStruct((n_rows, f), BF16),
        compiler_params=_params(1),
    )(sched["it_blk"], sched["it_col"], sched["it_exp"], sched["it_flag"], xs, w_gu, w_gu, b_gu4, b_gu4)

    def blk(i, nu):
        return jnp.minimum(i, nu[0] - 1)

    down_spec = pltpu.PrefetchScalarGridSpec(
        num_scalar_prefetch=3,
        grid=(nb,),
        in_specs=[pl.BlockSpec((tb, f), lambda i, be, nu, fi: (blk(i, nu), 0)),
                  pl.BlockSpec((None, None, f, d), lambda i, be, nu, fi: (layer, be[i], 0, 0)),
                  pl.BlockSpec((None, None, 1, d), lambda i, be, nu, fi: (layer, be[i], 0, 0))],
        out_specs=pl.BlockSpec((tb * chunks, LANES), lambda i, be, nu, fi: (i, 0)),
        scratch_shapes=[pltpu.VMEM((f, d), BF16)],
    )
    return pl.pallas_call(
        _expert_down_kernel,
        grid_spec=down_spec,
        out_shape=jax.ShapeDtypeStruct((n_rows * chunks, LANES), U32),
        compiler_params=_params(1),
    )(sched["blk_e"], sched["n_used"], sched["blk_first"], act, w_dn, b_dn.reshape(b_dn.shape[0], e, 1, d))


def _expert_schedule(total, n_rows, e, f):
    tb = EXPERT_ROWS
    nh = f // EXPERT_UP_COLS
    nb = n_rows // tb
    padded = (total + tb - 1) // tb * tb
    pad_end = jnp.cumsum(padded)
    pad_start = pad_end - padded
    n_used = pad_end[-1] // tb
    blocks = jnp.arange(nb, dtype=I32)
    blk_e = jnp.minimum(jnp.sum(pad_end[None, :] <= (blocks * tb)[:, None], axis=1), e - 1).astype(I32)
    blk_e = jnp.where(blocks < n_used, blk_e, blk_e[n_used - 1])
    blk_first = jnp.concatenate([jnp.ones((1,), I32), (blk_e[1:] != blk_e[:-1]).astype(I32)])
    grp_last = blocks == pad_end[blk_e] // tb - 1
    blk_fill = jnp.where(blocks < n_used, jnp.where(grp_last, (padded - total)[blk_e], 0), tb).astype(I32)
    items = jnp.arange(nh * nb, dtype=I32)
    grp_end = nh * (pad_end // tb)
    it_e = jnp.minimum(jnp.sum(grp_end[None, :] <= items[:, None], axis=1), e - 1).astype(I32)
    grp_blocks = jnp.maximum((padded // tb)[it_e], 1)
    r = items - nh * (pad_start // tb)[it_e]
    valid = items < nh * n_used
    spare = items - nh * n_used
    it_blk = jnp.where(valid, (pad_start // tb)[it_e] + r % grp_blocks, n_used + spare // nh)
    it_col = jnp.where(valid, r // grp_blocks, spare % nh)
    it_exp = jnp.where(valid, it_e, blk_e[n_used - 1])
    it_flag = jnp.where(valid, 1 + 2 * (r % grp_blocks == 0).astype(I32), 0)
    it_flag = it_flag.at[0].set(it_flag[0] | 2)
    return dict(pad_start=pad_start, blk_fill=blk_fill, blk_e=blk_e, n_used=n_used.astype(I32).reshape(1),
                blk_first=blk_first, it_blk=it_blk.astype(I32), it_col=it_col.astype(I32),
                it_exp=it_exp.astype(I32), it_flag=it_flag.astype(I32))


def _combine_kernel(pos_ref, nxt_ref, ys_ref, x1_ref, gcol_ref, p_ref, g_ref, b_ref, wpg_ref, wpp_ref, out_ref,
                    buf_ref, sem, *, alpha):
    tm = x1_ref.shape[0]
    chunks = buf_ref.shape[2] // tm
    i = pl.program_id(0)
    slot = i % 2
    d = out_ref.shape[1]

    def row_copy(rows_ref, t, kk, dst):
        r = rows_ref[0, 0, kk * tm + t]
        first = t * chunks if isinstance(t, int) else pl.multiple_of(t * chunks, chunks)
        return pltpu.make_async_copy(ys_ref.at[pl.ds(pl.multiple_of(r * chunks, chunks), chunks)],
                                     buf_ref.at[dst, kk, pl.ds(first, chunks)], sem.at[dst])

    def wait_rows(dst):
        for kk in range(TOP_K):
            pltpu.make_async_copy(ys_ref.at[pl.ds(0, tm * chunks)], buf_ref.at[dst, kk], sem.at[dst]).wait()

    @pl.when(i == 0)
    def _():
        def issue(g, carry):
            for u in range(DMA_UNROLL):
                for kk in range(TOP_K):
                    row_copy(pos_ref, pl.multiple_of(g * DMA_UNROLL, DMA_UNROLL) + u, kk, 0).start(priority=(u + kk) % 2)
            return carry

        lax.fori_loop(0, tm // DMA_UNROLL, issue, 0)

    wait_rows(slot)

    copies = [(t, kk) for t in range(tm) for kk in range(TOP_K)]
    n_stages = d // MXU_COLS
    per_stage = -(-len(copies) // n_stages)

    def start_next(stage):
        for j, (t, kk) in enumerate(copies[stage * per_stage:(stage + 1) * per_stage]):
            row_copy(nxt_ref, t, kk, 1 - slot).start(priority=j % 2)

    gates = gcol_ref[...]
    lo = hi = None
    for kk in range(TOP_K):
        gk = gates[:, kk:kk + 1]
        halves = [_unpack_halves(u) for u in _load_row_tiles(buf_ref, tm, (slot, kk))]
        lo = [gk * l for l, _ in halves] if lo is None else [a + gk * l for a, (l, _) in zip(lo, halves)]
        hi = [gk * h for _, h in halves] if hi is None else [a + gk * h for a, (_, h) in zip(hi, halves)]
    ffn = jnp.concatenate(lo + hi, axis=1)
    x2 = _layer_norm(alpha * x1_ref[...] + ffn, g_ref[...], b_ref[...])
    x2b = x2.astype(BF16)
    pb = p_ref[...].astype(BF16)
    for c in range(d // MXU_COLS):
        start_next(c)
        cs = slice(c * MXU_COLS, (c + 1) * MXU_COLS)
        gate = jax.nn.sigmoid(_dot(x2b, wpg_ref[:, cs]))
        out_ref[:, cs] = x2[:, cs] + gate * _dot(pb, wpp_ref[:, cs])

    @pl.when(i == pl.num_programs(0) - 1)
    def _():
        wait_rows(1 - slot)


def _combine(pos_blocked, ys, x1, gcol, p, ln_g, ln_b, w_pg, w_pp, alpha):
    n, d = x1.shape
    tm = COMBINE_TM
    pd = p.shape[1]
    full = lambda i: (0, 0)
    once = pl.Buffered(1)
    last = n // tm - 1
    return pl.pallas_call(
        functools.partial(_combine_kernel, alpha=alpha),
        grid=(n // tm,),
        in_specs=[pl.BlockSpec((1, 1, TOP_K * tm), lambda i: (i, 0, 0), memory_space=pltpu.SMEM),
                  pl.BlockSpec((1, 1, TOP_K * tm), lambda i: (jnp.minimum(i + 1, last), 0, 0),
                               memory_space=pltpu.SMEM),
                  pl.BlockSpec(memory_space=pl.ANY),
                  pl.BlockSpec((tm, d), lambda i: (i, 0)),
                  pl.BlockSpec((tm, LANES), lambda i: (i, 0)),
                  pl.BlockSpec((tm, pd), lambda i: (i, 0)),
                  pl.BlockSpec((1, d), full),
                  pl.BlockSpec((1, d), full),
                  pl.BlockSpec((d, d), full, pipeline_mode=once),
                  pl.BlockSpec((pd, d), full, pipeline_mode=once)],
        out_specs=pl.BlockSpec((tm, d), lambda i: (i, 0)),
        out_shape=jax.ShapeDtypeStruct((n, d), F32),
        scratch_shapes=[pltpu.VMEM((2, TOP_K, tm * (d // 2 // LANES), LANES), U32), pltpu.SemaphoreType.DMA((2,))],
        compiler_params=_params(1),
    )(pos_blocked, pos_blocked, ys, x1, gcol, p, ln_g.reshape(1, d), ln_b.reshape(1, d), w_pg, w_pp)


def _moe_tail(layer, x1, x1p, topi, gcol, cnt, p, ln_g, ln_b, w_gu, b_gu, w_dn, b_dn, w_pg, w_pp, alpha):
    n, d = x1.shape
    e = w_gu.shape[1]
    e_pad = cnt.shape[1]
    tb = EXPERT_ROWS
    c = cnt[:, :, 0].astype(I32)
    c = c.reshape(n // POS_TM, POS_TM // POST_TM, e_pad).sum(axis=1)
    n_rows = -(-n * TOP_K // tb) * tb + e * tb
    sched = _expert_schedule(c.sum(axis=0), n_rows, e, w_gu.shape[3] // 2)
    blk_off = (sched["pad_start"][None, :] + jnp.cumsum(c, axis=0) - c).astype(F32)[:, :, None]

    pos = _positions(topi, blk_off, e_pad)
    xs = _dispatch(sched["blk_fill"], _blocked_pos(pos, DISPATCH_TM), x1p, n_rows, d // 2 // LANES)
    ys = _experts(layer, sched, xs, w_gu, b_gu, w_dn, b_dn)
    return _combine(_blocked_pos(pos, COMBINE_TM), ys, x1, gcol, p, ln_g, ln_b, w_pg, w_pp, alpha)


def kernel(x, p, ln_g, ln_b, w_in_a, b_gate_a, norm_a, w_out_a, w_in_b, conv_b, w_out_b, w_router, b_router,
           w_gu, b_gu, w_dn, b_dn, w_ple_gate, w_ple_proj):
    bsz, seq, d = x.shape
    depth = ln_g.shape[0]
    n = bsz * seq
    alpha = (2 * depth) ** 0.25
    heads = b_gate_a.shape[-1] // 2
    dv = d // heads
    dk = dv // 2
    qkvo = 2 * heads * dk + 2 * heads * dv
    xf = x.reshape(n, d)
    for i in range(depth):
        j = i // 2
        if i % 2 == 0:
            col, row = _mlstm_gates(xf, w_in_a[j][:, qkvo:], b_gate_a[j], heads)
            proj = _proj(xf, w_in_a[j][:, :qkvo].astype(BF16), PROJ_TM, PROJ_TN_A)
            a = _mlstm(proj, col, row, norm_a[j], bsz, seq, heads, dk, dv)
            w_out = w_out_a[j]
        else:
            a = _conv_front(xf, w_in_b[j].astype(BF16), conv_b[j], seq)
            w_out = w_out_b[j]
        x1, x1p, topi, gcol, cnt = _post_mixer(a, w_out.astype(BF16), xf, ln_g[i, 0], ln_b[i, 0],
                                               w_router[i], b_router[i], alpha)
        xf = _moe_tail(i, x1, x1p, topi, gcol, cnt, p[i].reshape(n, -1), ln_g[i, 1], ln_b[i, 1],
                       w_gu, b_gu, w_dn, b_dn,
                       w_ple_gate[i].astype(BF16), w_ple_proj[i].astype(BF16), alpha)
    return xf.reshape(bsz, seq, d)
```

```python
import functools

import jax
import jax.numpy as jnp
from jax import lax
from jax.experimental import pallas as pl
from jax.experimental.pallas import tpu as pltpu

F32 = jnp.float32
BF16 = jnp.bfloat16
U32 = jnp.uint32
I32 = jnp.int32

GATE_SOFTCAP = 15.0
CONV_WIDTH = 3
TOP_K = 4
SWIGLU_ALPHA = 1.702
SWIGLU_LIMIT = 7.0
LN_EPS = 1e-5
RMS_EPS = 1e-6

LANES = 128
SUBLANES = 8
MXU_COLS = 256
VMEM_LIMIT_BYTES = 60000 * 1024

MLSTM_ROWS = 256
PROJ_TM = 1024
PROJ_TN_A = 1024
PROJ_TN_B = 512
POST_TM = 512
POS_TM = 512
DISPATCH_TM = 1024
EXPERT_ROWS = 512
EXPERT_UP_COLS = 1024
COMBINE_TM = 256
DMA_UNROLL = 8


def _dot(a, b):
    return jnp.dot(a, b, preferred_element_type=F32)


def _dot_nt(a, b):
    return lax.dot_general(a, b, (((1,), (1,)), ((), ())), preferred_element_type=F32)


def _dot_tn(a, b):
    return lax.dot_general(a, b, (((0,), (0,)), ((), ())), preferred_element_type=F32)


def _split2(x):
    hi = x.astype(BF16)
    lo = (x - hi.astype(F32)).astype(BF16)
    return hi, lo


def _split3(x):
    hi = x.astype(BF16)
    r = x - hi.astype(F32)
    mid = r.astype(BF16)
    lo = (r - mid.astype(F32)).astype(BF16)
    return hi, mid, lo


def _pack_halves(y):
    half = y.shape[1] // 2
    lo = lax.bitcast_convert_type(y[:, :half].astype(BF16).astype(F32), U32) >> 16
    hi = lax.bitcast_convert_type(y[:, half:].astype(BF16).astype(F32), U32) & jnp.uint32(0xFFFF0000)
    return lo | hi


def _unpack_halves(u):
    lo = lax.bitcast_convert_type(u << 16, F32)
    hi = lax.bitcast_convert_type(u & jnp.uint32(0xFFFF0000), F32)
    return lo, hi


def _store_row_tiles(ref, words, first=0, block_rows=None):
    r = words.shape[0]
    c = ref.shape[0] // r if block_rows is None else ref.shape[0] // block_rows
    for j in range(words.shape[1] // LANES):
        ref[pl.ds(first + j, r, stride=c), :] = words[:, j * LANES:(j + 1) * LANES]


def _load_row_tiles(ref, r, index=(), take=None):
    c = ref.shape[-2] // r
    return [ref[index + (pl.ds(j, take or r, stride=c), slice(None))] for j in range(c)]


def _layer_norm(y, g, b):
    mu = jnp.mean(y, axis=-1, keepdims=True)
    yc = y - mu
    var = jnp.mean(yc * yc, axis=-1, keepdims=True)
    return yc * lax.rsqrt(var + LN_EPS) * g + b


def _params(n_axes):
    return pltpu.CompilerParams(dimension_semantics=("arbitrary",) * n_axes,
                                vmem_limit_bytes=VMEM_LIMIT_BYTES)


def _proj_gate_kernel(x_ref, w_ref, wg_ref, bg_ref, o_ref, col_ref, row_ref, xb_ref, *, heads, rows):
    tm = x_ref.shape[0]

    @pl.when(pl.program_id(1) == 0)
    def _():
        x = x_ref[...]
        xh = x.astype(BF16)
        xb_ref[...] = xh
        xl = (x - xh.astype(F32)).astype(BF16)
        wh, wl = _split2(wg_ref[...])
        g = _dot(xh, wh) + _dot(xh, wl) + _dot(xl, wh) + bg_ref[...]
        g = GATE_SOFTCAP * jnp.tanh(g / GATE_SOFTCAP)
        lf = -jnp.log1p(jnp.exp(-g))
        r_i = lax.broadcasted_iota(I32, (rows, rows), 0)
        c_i = lax.broadcasted_iota(I32, (rows, rows), 1)
        tril = (c_i <= r_i).astype(BF16)
        lane = lax.broadcasted_iota(I32, (rows, LANES), 1)
        for c in range(tm // rows):
            rs = slice(c * rows, (c + 1) * rows)
            p0, p1, p2 = _split3(lf[rs])
            bsum = _dot(tril, p0) + _dot(tril, p1) + _dot(tril, p2)
            col = jnp.where(lane < heads, g[rs], jnp.where(lane < 2 * heads, bsum, 0.0))
            col_ref[rs, :] = col
            row_ref[:, rs] = col.T[: row_ref.shape[0]]

    o_ref[...] = _dot(xb_ref[...], w_ref[...]).astype(o_ref.dtype)


def _proj_gates(xf, w, w_g, b_g, heads):
    n, d = xf.shape
    m = w.shape[1]
    tm, tn = PROJ_TM, PROJ_TN_A
    while m % tn:
        tn //= 2
    r_pad = -(-2 * heads // SUBLANES) * SUBLANES
    wg = jnp.zeros((d, LANES), F32).at[:, : 2 * heads].set(w_g)
    bg = jnp.zeros((1, LANES), F32).at[0, : 2 * heads].set(b_g)
    return pl.pallas_call(
        functools.partial(_proj_gate_kernel, heads=heads, rows=MLSTM_ROWS),
        grid=(n // tm, m // tn),
        in_specs=[pl.BlockSpec((tm, d), lambda i, j: (i, 0)),
                  pl.BlockSpec((d, tn), lambda i, j: (0, j)),
                  pl.BlockSpec((d, LANES), lambda i, j: (0, 0)),
                  pl.BlockSpec((1, LANES), lambda i, j: (0, 0))],
        out_specs=[pl.BlockSpec((tm, tn), lambda i, j: (i, j)),
                   pl.BlockSpec((tm, LANES), lambda i, j: (i, 0)),
                   pl.BlockSpec((r_pad, tm), lambda i, j: (0, i))],
        out_shape=[jax.ShapeDtypeStruct((n, m), BF16),
                   jax.ShapeDtypeStruct((n, LANES), F32),
                   jax.ShapeDtypeStruct((r_pad, n), F32)],
        scratch_shapes=[pltpu.VMEM((tm, d), BF16)],
        compiler_params=_params(2),
    )(xf, w, wg, bg)


def _mlstm_kernel(q_ref, k_ref, v_ref, o_ref, col_ref, row_ref, norm_ref, out_ref, c_ref, n_ref, *, heads, dk, dv):
    rows = q_ref.shape[0]
    scale = dk ** -0.5

    @pl.when(pl.program_id(1) == 0)
    def _():
        c_ref[...] = jnp.zeros_like(c_ref)
        n_ref[...] = jnp.zeros_like(n_ref)

    col = col_ref[...]
    row = row_ref[...]
    t_i = lax.broadcasted_iota(I32, (rows, rows), 0)
    s_i = lax.broadcasted_iota(I32, (rows, rows), 1)
    causal = s_i <= t_i
    for h in range(heads):
        q = q_ref[:, h * dk:(h + 1) * dk]
        k = k_ref[:, h * dk:(h + 1) * dk]
        v = v_ref[:, h * dv:(h + 1) * dv]
        li_c = col[:, h:h + 1]
        b_c = col[:, heads + h:heads + h + 1]
        li_r = row[h:h + 1, :]
        b_r = row[heads + h:heads + h + 1, :]
        b_last = b_c[rows - 1:rows, :]
        decay_w = jnp.where(causal, jnp.exp(b_c - b_r + li_r), 0.0)
        sw = _dot_nt(q, k) * scale * decay_w
        eb = jnp.exp(b_c) * scale
        c_prev = c_ref[h]
        n_prev = n_ref[h]
        num = _dot(sw.astype(BF16), v) + eb * _dot(q, c_prev.astype(BF16))
        qn = jnp.sum(q.astype(F32) * n_prev, axis=-1, keepdims=True)
        den = jnp.sum(sw, axis=-1, keepdims=True) + eb * qn
        hh = num / jnp.maximum(jnp.abs(den), 1.0)
        wk = jnp.exp(b_last - b_c + li_c)
        carry = jnp.exp(b_last)
        kf = k.astype(F32) * wk
        c_ref[h] = carry * c_prev + _dot_tn(kf.astype(BF16), v)
        n_ref[h] = carry * n_prev + jnp.sum(kf, axis=0, keepdims=True)
        hh = hh * lax.rsqrt(jnp.mean(hh * hh, axis=-1, keepdims=True) + RMS_EPS)
        hh = hh * norm_ref[:, h * dv:(h + 1) * dv]
        og = o_ref[:, h * dv:(h + 1) * dv].astype(F32)
        out_ref[:, h * dv:(h + 1) * dv] = (hh * jax.nn.sigmoid(og)).astype(out_ref.dtype)


def _mlstm(proj, col, row, norm, bsz, seq, heads, dk, dv):
    rows = MLSTM_ROWS
    nc = seq // rows
    qk_w, v_w = heads * dk, heads * dv
    assert v_w == 2 * qk_w
    r_pad = row.shape[0]
    tok = lambda b, c: b * nc + c
    return pl.pallas_call(
        functools.partial(_mlstm_kernel, heads=heads, dk=dk, dv=dv),
        grid=(bsz, nc),
        in_specs=[pl.BlockSpec((rows, qk_w), lambda b, c: (tok(b, c), 0)),
                  pl.BlockSpec((rows, qk_w), lambda b, c: (tok(b, c), 1)),
                  pl.BlockSpec((rows, v_w), lambda b, c: (tok(b, c), 1)),
                  pl.BlockSpec((rows, v_w), lambda b, c: (tok(b, c), 2)),
                  pl.BlockSpec((rows, LANES), lambda b, c: (tok(b, c), 0)),
                  pl.BlockSpec((r_pad, rows), lambda b, c: (0, tok(b, c))),
                  pl.BlockSpec((1, v_w), lambda b, c: (0, 0))],
        out_specs=pl.BlockSpec((rows, v_w), lambda b, c: (tok(b, c), 0)),
        out_shape=jax.ShapeDtypeStruct((bsz * seq, v_w), BF16),
        scratch_shapes=[pltpu.VMEM((heads, dk, dv), F32), pltpu.VMEM((heads, 1, dk), F32)],
        compiler_params=_params(2),
    )(proj, proj, proj, proj, col, row, norm.reshape(1, v_w))


def _conv_kernel(x_ref, wb_ref, wc_ref, wu_ref, cw_ref, o_ref, xb_ref, carry_ref, *, steps_per_seq):
    i, j = pl.program_id(0), pl.program_id(1)
    tm = x_ref.shape[0]

    @pl.when(j == 0)
    def _():
        xb_ref[...] = x_ref[...].astype(BF16)

    xb = xb_ref[...]
    z = _dot(xb, wc_ref[...]) * _dot(xb, wu_ref[...])
    @pl.when(i % steps_per_seq == 0)
    def _():
        carry_ref[j] = jnp.zeros(carry_ref.shape[1:], F32)

    prev = carry_ref[j]
    carry_ref[j] = z[tm - SUBLANES:, :]
    r_i = lax.broadcasted_iota(I32, z.shape, 0)
    z1 = jnp.where(r_i == 0, prev[SUBLANES - 1:SUBLANES, :], pltpu.roll(z, 1, 0))
    z2 = jnp.where(r_i == 0, prev[SUBLANES - 2:SUBLANES - 1, :],
                   jnp.where(r_i == 1, prev[SUBLANES - 1:SUBLANES, :], pltpu.roll(z, 2, 0)))
    cw = cw_ref[...]
    zc = cw[0:1, :] * z2 + cw[1:2, :] * z1 + cw[2:3, :] * z
    o_ref[...] = (_dot(xb, wb_ref[...]) * zc).astype(o_ref.dtype)


def _conv_front(xf, w_in, conv_w, seq):
    n, d = xf.shape
    tm, tn = PROJ_TM, PROJ_TN_B
    nj = d // tn
    cw = jnp.zeros((SUBLANES, d), F32).at[:CONV_WIDTH].set(conv_w)
    return pl.pallas_call(
        functools.partial(_conv_kernel, steps_per_seq=seq // tm),
        grid=(n // tm, nj),
        in_specs=[pl.BlockSpec((tm, d), lambda i, j: (i, 0)),
                  pl.BlockSpec((d, tn), lambda i, j: (0, j)),
                  pl.BlockSpec((d, tn), lambda i, j: (0, nj + j)),
                  pl.BlockSpec((d, tn), lambda i, j: (0, 2 * nj + j)),
                  pl.BlockSpec((SUBLANES, tn), lambda i, j: (0, j))],
        out_specs=pl.BlockSpec((tm, tn), lambda i, j: (i, j)),
        out_shape=jax.ShapeDtypeStruct((n, d), BF16),
        scratch_shapes=[pltpu.VMEM((tm, d), BF16), pltpu.VMEM((nj, SUBLANES, tn), F32)],
        compiler_params=_params(2),
    )(xf, w_in, w_in, w_in, cw)


def _post_kernel(a_ref, w_ref, x_ref, g_ref, b_ref, wr_ref, br_ref,
                 x1_ref, x1p_ref, topi_ref, gcol_ref, cnt_ref, *, alpha, e_pad):
    tm = a_ref.shape[0]
    y = alpha * x_ref[...] + _dot(a_ref[...], w_ref[...])
    x1 = _layer_norm(y, g_ref[...], b_ref[...])
    x1_ref[...] = x1
    _store_row_tiles(x1p_ref, _pack_halves(x1))
    xh, xl = _split2(x1)
    w2 = wr_ref[...]
    l2 = _dot(xh, w2)
    logits = l2[:, :LANES] + l2[:, LANES:] + _dot(xl, w2[:, :LANES]) + br_ref[...]
    lt = logits.T[:e_pad]
    e_i = lax.broadcasted_iota(I32, lt.shape, 0)
    cur = lt
    ids, vals = [], []
    for _ in range(TOP_K):
        m = jnp.max(cur, axis=0, keepdims=True)
        idx = jnp.min(jnp.where(cur == m, e_i, e_pad), axis=0, keepdims=True)
        cur = jnp.where(e_i == idx, -jnp.inf, cur)
        ids.append(idx)
        vals.append(m)
    ex = [jnp.exp(v - vals[0]) for v in vals]
    den = ex[0] + ex[1] + ex[2] + ex[3]
    k8 = lax.broadcasted_iota(I32, (SUBLANES, tm), 0)
    k128 = lax.broadcasted_iota(I32, (LANES, tm), 0)
    topi = jnp.zeros((SUBLANES, tm), I32)
    gmat = jnp.zeros((LANES, tm), F32)
    for kk in range(TOP_K):
        topi = jnp.where(k8 == kk, ids[kk], topi)
        gmat = jnp.where(k128 == kk, ex[kk] / den, gmat)
    topi_ref[...] = topi
    gcol_ref[...] = gmat.T
    sel = (cur == -jnp.inf).astype(F32)
    cnt_ref[0] = jnp.broadcast_to(jnp.sum(sel, axis=1, keepdims=True), (e_pad, LANES))


def _post_mixer(a, w_out, xf, ln_g, ln_b, w_router, b_router, alpha):
    n, d = xf.shape
    tm = POST_TM
    e = w_router.shape[1]
    assert TOP_K <= e <= LANES
    e_pad = -(-e // SUBLANES) * SUBLANES
    wr = jnp.zeros((d, LANES), F32).at[:, :e].set(w_router)
    wr_hi = wr.astype(BF16)
    wr_lo = (wr - wr_hi.astype(F32)).astype(BF16)
    wr2 = jnp.concatenate([wr_hi, wr_lo], axis=1)
    br = jnp.full((1, LANES), -1e30, F32).at[0, :e].set(b_router)
    full = lambda i: (0, 0)
    once = pl.Buffered(1)
    chunks = d // 2 // LANES
    return pl.pallas_call(
        functools.partial(_post_kernel, alpha=alpha, e_pad=e_pad),
        grid=(n // tm,),
        in_specs=[pl.BlockSpec((tm, d), lambda i: (i, 0)),
                  pl.BlockSpec((d, d), full, pipeline_mode=once),
                  pl.BlockSpec((tm, d), lambda i: (i, 0)),
                  pl.BlockSpec((1, d), full),
                  pl.BlockSpec((1, d), full),
                  pl.BlockSpec((d, 2 * LANES), full, pipeline_mode=once),
                  pl.BlockSpec((1, LANES), full)],
        out_specs=[pl.BlockSpec((tm, d), lambda i: (i, 0)),
                   pl.BlockSpec((tm * chunks, LANES), lambda i: (i, 0)),
                   pl.BlockSpec((SUBLANES, tm), lambda i: (0, i)),
                   pl.BlockSpec((tm, LANES), lambda i: (i, 0)),
                   pl.BlockSpec((1, e_pad, LANES), lambda i: (i, 0, 0))],
        out_shape=[jax.ShapeDtypeStruct((n, d), F32),
                   jax.ShapeDtypeStruct((n * chunks, LANES), U32),
                   jax.ShapeDtypeStruct((SUBLANES, n), I32),
                   jax.ShapeDtypeStruct((n, LANES), F32),
                   jax.ShapeDtypeStruct((n // tm, e_pad, LANES), F32)],
        compiler_params=_params(1),
    )(a, w_out, xf, ln_g.reshape(1, d), ln_b.reshape(1, d), wr2, br)


def _pos_kernel(topi_ref, off_ref, pos_ref, *, e_pad):
    tm = topi_ref.shape[1]
    ti = topi_ref[...]
    e_i = lax.broadcasted_iota(I32, (e_pad, tm), 0)
    hits = [e_i == ti[kk:kk + 1, :] for kk in range(TOP_K)]
    sel = hits[0] | hits[1] | hits[2] | hits[3]
    s_i = lax.broadcasted_iota(I32, (tm, tm), 0)
    t_i = lax.broadcasted_iota(I32, (tm, tm), 1)
    before = (s_i < t_i).astype(BF16)
    cum = _dot(sel.astype(BF16), before) + off_ref[0]
    k8 = lax.broadcasted_iota(I32, (SUBLANES, tm), 0)
    pos = jnp.zeros((SUBLANES, tm), F32)
    for kk in range(TOP_K):
        pk = jnp.sum(jnp.where(hits[kk], cum, 0.0), axis=0, keepdims=True)
        pos = jnp.where(k8 == kk, pk, pos)
    pos_ref[...] = pos.astype(I32)


def _positions(topi, blk_off, e_pad):
    n = topi.shape[1]
    tm = POS_TM
    return pl.pallas_call(
        functools.partial(_pos_kernel, e_pad=e_pad),
        grid=(n // tm,),
        in_specs=[pl.BlockSpec((SUBLANES, tm), lambda i: (0, i)),
                  pl.BlockSpec((1, e_pad, 1), lambda i: (i, 0, 0))],
        out_specs=pl.BlockSpec((SUBLANES, tm), lambda i: (0, i)),
        out_shape=jax.ShapeDtypeStruct((SUBLANES, n), I32),
        compiler_params=_params(1),
    )(topi, blk_off)


def _blocked_pos(pos, tm):
    n = pos.shape[1]
    p = pos[:TOP_K].reshape(TOP_K, n // tm, tm)
    return jnp.transpose(p, (1, 0, 2)).reshape(n // tm, 1, TOP_K * tm)


def _dispatch_kernel(fill_ref, pos_ref, x_ref, xs_ref, zero_ref, sem, fill_sem, *, chunks):
    tm = x_ref.shape[0] // chunks
    tb = zero_ref.shape[0] // chunks

    def rows_of(ref, first, count):
        return ref.at[pl.ds(pl.multiple_of(first * chunks, chunks), count * chunks)]

    @pl.when(pl.program_id(0) == 0)
    def _():
        zero_ref[...] = jnp.zeros_like(zero_ref)

        def each_copy(b, action):
            cnt = fill_ref[b]
            row = (b + 1) * tb - cnt
            bit = tb
            while bit:
                @pl.when((cnt & bit) != 0)
                def _(row=row, bit=bit):
                    action(pltpu.make_async_copy(rows_of(zero_ref, 0, bit), rows_of(xs_ref, row, bit), fill_sem))

                row = row + (cnt & bit)
                bit //= 2

        def start_all(b, carry):
            each_copy(b, lambda cp: cp.start())
            return carry

        def wait_all(b, carry):
            each_copy(b, lambda cp: cp.wait())
            return carry

        lax.fori_loop(0, fill_ref.shape[0], start_all, 0)
        lax.fori_loop(0, fill_ref.shape[0], wait_all, 0)

    def issue(g, carry):
        for u in range(DMA_UNROLL):
            t = pl.multiple_of(g * DMA_UNROLL, DMA_UNROLL) + u
            for kk in range(TOP_K):
                p = pos_ref[0, 0, kk * tm + t]
                pltpu.make_async_copy(rows_of(x_ref, t, 1), rows_of(xs_ref, p, 1), sem).start(priority=(u + kk) % 2)
        return carry

    lax.fori_loop(0, tm // DMA_UNROLL, issue, 0)
    for _ in range(TOP_K):
        pltpu.make_async_copy(x_ref, rows_of(xs_ref, 0, tm), sem).wait()


def _dispatch(blk_fill, pos_blocked, x1p, n_rows, chunks):
    n = x1p.shape[0] // chunks
    tm = DISPATCH_TM
    grid_spec = pltpu.PrefetchScalarGridSpec(
        num_scalar_prefetch=1,
        grid=(n // tm,),
        in_specs=[pl.BlockSpec((1, 1, TOP_K * tm), lambda i, fl: (i, 0, 0), memory_space=pltpu.SMEM),
                  pl.BlockSpec((tm * chunks, LANES), lambda i, fl: (i, 0))],
        out_specs=pl.BlockSpec(memory_space=pl.ANY),
        scratch_shapes=[pltpu.VMEM((EXPERT_ROWS * chunks, LANES), U32), pltpu.SemaphoreType.DMA(()),
                        pltpu.SemaphoreType.DMA(())],
    )
    return pl.pallas_call(
        functools.partial(_dispatch_kernel, chunks=chunks),
        grid_spec=grid_spec,
        out_shape=jax.ShapeDtypeStruct((n_rows * chunks, LANES), U32),
        compiler_params=_params(1),
    )(blk_fill, pos_blocked, x1p)


def _expert_up_kernel(blk_ref, col_ref, exp_ref, flag_ref, xs_ref, wg_ref, wu_ref, bg_ref, bu_ref, act_ref,
                      wgb_ref, wub_ref):
    del blk_ref, col_ref, exp_ref
    flag = flag_ref[pl.program_id(0)]
    tb = act_ref.shape[0]

    @pl.when((flag & 2) != 0)
    def _():
        wgb_ref[...] = wg_ref[...].astype(BF16)
        wub_ref[...] = wu_ref[...].astype(BF16)

    def compute(rows):
        halves = [_unpack_halves(u) for u in _load_row_tiles(xs_ref, tb, take=rows)]
        xb = jnp.concatenate([lo.astype(BF16) for lo, _ in halves] + [hi.astype(BF16) for _, hi in halves], axis=1)
        for c in range(0, act_ref.shape[1], MXU_COLS):
            cs = slice(c, c + MXU_COLS)
            g = jnp.minimum(_dot(xb, wgb_ref[:, cs]) + bg_ref[:, cs], SWIGLU_LIMIT)
            u = jnp.clip(_dot(xb, wub_ref[:, cs]) + bu_ref[:, cs], -SWIGLU_LIMIT, SWIGLU_LIMIT)
            act_ref[:rows, cs] = ((u + 1.0) * (g * jax.nn.sigmoid(SWIGLU_ALPHA * g))).astype(act_ref.dtype)

    @pl.when((flag & 5) == 1)
    def _():
        compute(tb)

    @pl.when((flag & 5) == 5)
    def _():
        compute(tb // 2)
        act_ref[tb // 2:, :] = jnp.zeros((tb - tb // 2, act_ref.shape[1]), act_ref.dtype)

    @pl.when((flag & 1) == 0)
    def _():
        act_ref[...] = jnp.zeros_like(act_ref)


def _expert_down_kernel(be_ref, nu_ref, first_ref, act_ref, wd_ref, bd_ref, ys_ref, wdb_ref):
    del be_ref
    i = pl.program_id(0)

    flag = first_ref[i]
    tb = act_ref.shape[0]

    @pl.when((flag & 1) != 0)
    def _():
        wdb_ref[...] = wd_ref[...].astype(BF16)

    def compute(rows):
        act = act_ref[:rows, :]
        half = wdb_ref.shape[1] // 2
        for c in range(0, half, MXU_COLS):
            lo = _dot(act, wdb_ref[:, c:c + MXU_COLS]) + bd_ref[:, c:c + MXU_COLS]
            hi = _dot(act, wdb_ref[:, half + c:half + c + MXU_COLS]) + bd_ref[:, half + c:half + c + MXU_COLS]
            _store_row_tiles(ys_ref, _pack_halves(jnp.concatenate([lo, hi], axis=1)), first=c // LANES,
                             block_rows=tb)

    @pl.when((i < nu_ref[0]) & ((flag & 2) == 0))
    def _():
        compute(tb)

    @pl.when((i < nu_ref[0]) & ((flag & 2) != 0))
    def _():
        compute(tb // 2)
        tail = ys_ref.shape[0] // tb * (tb // 2)
        ys_ref[tail:, :] = jnp.zeros((ys_ref.shape[0] - tail, ys_ref.shape[1]), ys_ref.dtype)

    @pl.when(i >= nu_ref[0])
    def _():
        ys_ref[...] = jnp.zeros_like(ys_ref)


def _experts(layer, sched, xs, w_gu, b_gu, w_dn, b_dn):
    _, e, d, f2 = w_gu.shape
    chunks = d // 2 // LANES
    n_rows = xs.shape[0] // chunks
    f = f2 // 2
    tb, fh = EXPERT_ROWS, EXPERT_UP_COLS
    nh = f // fh
    nb = n_rows // tb
    up_spec = pltpu.PrefetchScalarGridSpec(
        num_scalar_prefetch=4,
        grid=(nh * nb,),
        in_specs=[pl.BlockSpec((tb * chunks, LANES), lambda t, bl, co, ex, fl: (bl[t], 0)),
                  pl.BlockSpec((None, None, d, fh), lambda t, bl, co, ex, fl: (layer, ex[t], 0, co[t])),
                  pl.BlockSpec((None, None, d, fh), lambda t, bl, co, ex, fl: (layer, ex[t], 0, nh + co[t])),
                  pl.BlockSpec((None, None, 1, fh), lambda t, bl, co, ex, fl: (layer, ex[t], 0, co[t])),
                  pl.BlockSpec((None, None, 1, fh), lambda t, bl, co, ex, fl: (layer, ex[t], 0, nh + co[t]))],
        out_specs=pl.BlockSpec((tb, fh), lambda t, bl, co, ex, fl: (bl[t], co[t])),
        scratch_shapes=[pltpu.VMEM((d, fh), BF16), pltpu.VMEM((d, fh), BF16)],
    )
    b_gu4 = b_gu.reshape(b_gu.shape[0], e, 1, f2)
    act = pl.pallas_call(
        _expert_up_kernel,
        grid_spec=up_spec,
        out_shape=jax.ShapeDtypeStruct((n_rows, f), BF16),
        compiler_params=_params(1),
    )(sched["it_blk"], sched["it_col"], sched["it_exp"], sched["it_flag"], xs, w_gu, w_gu, b_gu4, b_gu4)

    def blk(i, nu):
        return jnp.minimum(i, nu[0] - 1)

    down_spec = pltpu.PrefetchScalarGridSpec(
        num_scalar_prefetch=3,
        grid=(nb,),
        in_specs=[pl.BlockSpec((tb, f), lambda i, be, nu, fi: (blk(i, nu), 0)),
                  pl.BlockSpec((None, None, f, d), lambda i, be, nu, fi: (layer, be[i], 0, 0)),
                  pl.BlockSpec((None, None, 1, d), lambda i, be, nu, fi: (layer, be[i], 0, 0))],
        out_specs=pl.BlockSpec((tb * chunks, LANES), lambda i, be, nu, fi: (i, 0)),
        scratch_shapes=[pltpu.VMEM((f, d), BF16)],
    )
    return pl.pallas_call(
        _expert_down_kernel,
        grid_spec=down_spec,
        out_shape=jax.ShapeDtypeStruct((n_rows * chunks, LANES), U32),
        compiler_params=_params(1),
    )(sched["blk_e"], sched["n_used"], sched["blk_first"], act, w_dn, b_dn.reshape(b_dn.shape[0], e, 1, d))


def _expert_schedule(total, n_rows, e, f):
    tb = EXPERT_ROWS
    nh = f // EXPERT_UP_COLS
    nb = n_rows // tb
    padded = (total + tb - 1) // tb * tb
    pad_end = jnp.cumsum(padded)
    pad_start = pad_end - padded
    n_used = pad_end[-1] // tb
    blocks = jnp.arange(nb, dtype=I32)
    blk_e = jnp.minimum(jnp.sum(pad_end[None, :] <= (blocks * tb)[:, None], axis=1), e - 1).astype(I32)
    blk_e = jnp.where(blocks < n_used, blk_e, blk_e[n_used - 1])
    blk_first = jnp.concatenate([jnp.ones((1,), I32), (blk_e[1:] != blk_e[:-1]).astype(I32)])
    grp_last = blocks == pad_end[blk_e] // tb - 1
    blk_fill = jnp.where(blocks < n_used, jnp.where(grp_last, (padded - total)[blk_e], 0), tb).astype(I32)
    blk_half = ((blocks < n_used) & (blk_fill >= tb - tb // 2)).astype(I32)
    blk_first = blk_first + 2 * blk_half
    items = jnp.arange(nh * nb, dtype=I32)
    grp_end = nh * (pad_end // tb)
    it_e = jnp.minimum(jnp.sum(grp_end[None, :] <= items[:, None], axis=1), e - 1).astype(I32)
    grp_blocks = jnp.maximum((padded // tb)[it_e], 1)
    r = items - nh * (pad_start // tb)[it_e]
    valid = items < nh * n_used
    spare = items - nh * n_used
    it_blk = jnp.where(valid, (pad_start // tb)[it_e] + r % grp_blocks, n_used + spare // nh)
    it_col = jnp.where(valid, r // grp_blocks, spare % nh)
    it_exp = jnp.where(valid, it_e, blk_e[n_used - 1])
    it_flag = jnp.where(valid, 1 + 2 * (r % grp_blocks == 0).astype(I32) + 4 * blk_half[it_blk], 0)
    it_flag = it_flag.at[0].set(it_flag[0] | 2)
    return dict(pad_start=pad_start, blk_fill=blk_fill, blk_e=blk_e, n_used=n_used.astype(I32).reshape(1),
                blk_first=blk_first, it_blk=it_blk.astype(I32), it_col=it_col.astype(I32),
                it_exp=it_exp.astype(I32), it_flag=it_flag.astype(I32))


def _combine_kernel(pos_ref, nxt_ref, ys_ref, x1_ref, gcol_ref, p_ref, g_ref, b_ref, wpg_ref, wpp_ref, out_ref,
                    buf_ref, sem, *, alpha):
    tm = x1_ref.shape[0]
    chunks = buf_ref.shape[2] // tm
    i = pl.program_id(0)
    slot = i % 2
    d = out_ref.shape[1]

    def row_copy(rows_ref, t, kk, dst):
        r = rows_ref[0, 0, kk * tm + t]
        first = t * chunks if isinstance(t, int) else pl.multiple_of(t * chunks, chunks)
        return pltpu.make_async_copy(ys_ref.at[pl.ds(pl.multiple_of(r * chunks, chunks), chunks)],
                                     buf_ref.at[dst, kk, pl.ds(first, chunks)], sem.at[dst])

    def wait_rows(dst):
        for kk in range(TOP_K):
            pltpu.make_async_copy(ys_ref.at[pl.ds(0, tm * chunks)], buf_ref.at[dst, kk], sem.at[dst]).wait()

    @pl.when(i == 0)
    def _():
        def issue(g, carry):
            for u in range(DMA_UNROLL):
                for kk in range(TOP_K):
                    row_copy(pos_ref, pl.multiple_of(g * DMA_UNROLL, DMA_UNROLL) + u, kk, 0).start(priority=(u + kk) % 2)
            return carry

        lax.fori_loop(0, tm // DMA_UNROLL, issue, 0)

    wait_rows(slot)

    copies = [(t, kk) for t in range(tm) for kk in range(TOP_K)]
    n_stages = d // MXU_COLS
    per_stage = -(-len(copies) // n_stages)

    def start_next(stage):
        for j, (t, kk) in enumerate(copies[stage * per_stage:(stage + 1) * per_stage]):
            row_copy(nxt_ref, t, kk, 1 - slot).start(priority=j % 2)

    gates = gcol_ref[...]
    lo = hi = None
    for kk in range(TOP_K):
        gk = gates[:, kk:kk + 1]
        halves = [_unpack_halves(u) for u in _load_row_tiles(buf_ref, tm, (slot, kk))]
        lo = [gk * l for l, _ in halves] if lo is None else [a + gk * l for a, (l, _) in zip(lo, halves)]
        hi = [gk * h for _, h in halves] if hi is None else [a + gk * h for a, (_, h) in zip(hi, halves)]
    ffn = jnp.concatenate(lo + hi, axis=1)
    x2 = _layer_norm(alpha * x1_ref[...] + ffn, g_ref[...], b_ref[...])
    x2b = x2.astype(BF16)
    pb = p_ref[...].astype(BF16)
    for c in range(d // MXU_COLS):
        start_next(c)
        cs = slice(c * MXU_COLS, (c + 1) * MXU_COLS)
        gate = jax.nn.sigmoid(_dot(x2b, wpg_ref[:, cs]))
        out_ref[:, cs] = x2[:, cs] + gate * _dot(pb, wpp_ref[:, cs])

    @pl.when(i == pl.num_programs(0) - 1)
    def _():
        wait_rows(1 - slot)


def _combine(pos_blocked, ys, x1, gcol, p, ln_g, ln_b, w_pg, w_pp, alpha):
    n, d = x1.shape
    tm = COMBINE_TM
    pd = p.shape[1]
    full = lambda i: (0, 0)
    once = pl.Buffered(1)
    last = n // tm - 1
    return pl.pallas_call(
        functools.partial(_combine_kernel, alpha=alpha),
        grid=(n // tm,),
        in_specs=[pl.BlockSpec((1, 1, TOP_K * tm), lambda i: (i, 0, 0), memory_space=pltpu.SMEM),
                  pl.BlockSpec((1, 1, TOP_K * tm), lambda i: (jnp.minimum(i + 1, last), 0, 0),
                               memory_space=pltpu.SMEM),
                  pl.BlockSpec(memory_space=pl.ANY),
                  pl.BlockSpec((tm, d), lambda i: (i, 0)),
                  pl.BlockSpec((tm, LANES), lambda i: (i, 0)),
                  pl.BlockSpec((tm, pd), lambda i: (i, 0)),
                  pl.BlockSpec((1, d), full),
                  pl.BlockSpec((1, d), full),
                  pl.BlockSpec((d, d), full, pipeline_mode=once),
                  pl.BlockSpec((pd, d), full, pipeline_mode=once)],
        out_specs=pl.BlockSpec((tm, d), lambda i: (i, 0)),
        out_shape=jax.ShapeDtypeStruct((n, d), F32),
        scratch_shapes=[pltpu.VMEM((2, TOP_K, tm * (d // 2 // LANES), LANES), U32), pltpu.SemaphoreType.DMA((2,))],
        compiler_params=_params(1),
    )(pos_blocked, pos_blocked, ys, x1, gcol, p, ln_g.reshape(1, d), ln_b.reshape(1, d), w_pg, w_pp)


def _moe_tail(layer, x1, x1p, topi, gcol, cnt, p, ln_g, ln_b, w_gu, b_gu, w_dn, b_dn, w_pg, w_pp, alpha):
    n, d = x1.shape
    e = w_gu.shape[1]
    e_pad = cnt.shape[1]
    tb = EXPERT_ROWS
    c = cnt[:, :, 0].astype(I32)
    c = c.reshape(n // POS_TM, POS_TM // POST_TM, e_pad).sum(axis=1)
    n_rows = -(-n * TOP_K // tb) * tb + e * tb
    sched = _expert_schedule(c.sum(axis=0), n_rows, e, w_gu.shape[3] // 2)
    blk_off = (sched["pad_start"][None, :] + jnp.cumsum(c, axis=0) - c).astype(F32)[:, :, None]

    pos = _positions(topi, blk_off, e_pad)
    xs = _dispatch(sched["blk_fill"], _blocked_pos(pos, DISPATCH_TM), x1p, n_rows, d // 2 // LANES)
    ys = _experts(layer, sched, xs, w_gu, b_gu, w_dn, b_dn)
    return _combine(_blocked_pos(pos, COMBINE_TM), ys, x1, gcol, p, ln_g, ln_b, w_pg, w_pp, alpha)


def kernel(x, p, ln_g, ln_b, w_in_a, b_gate_a, norm_a, w_out_a, w_in_b, conv_b, w_out_b, w_router, b_router,
           w_gu, b_gu, w_dn, b_dn, w_ple_gate, w_ple_proj):
    bsz, seq, d = x.shape
    depth = ln_g.shape[0]
    n = bsz * seq
    alpha = (2 * depth) ** 0.25
    heads = b_gate_a.shape[-1] // 2
    dv = d // heads
    dk = dv // 2
    qkvo = 2 * heads * dk + 2 * heads * dv
    xf = x.reshape(n, d)
    for i in range(depth):
        j = i // 2
        if i % 2 == 0:
            proj, col, row = _proj_gates(xf, w_in_a[j][:, :qkvo].astype(BF16), w_in_a[j][:, qkvo:], b_gate_a[j],
                                         heads)
            a = _mlstm(proj, col, row, norm_a[j], bsz, seq, heads, dk, dv)
            w_out = w_out_a[j]
        else:
            a = _conv_front(xf, w_in_b[j].astype(BF16), conv_b[j], seq)
            w_out = w_out_b[j]
        x1, x1p, topi, gcol, cnt = _post_mixer(a, w_out.astype(BF16), xf, ln_g[i, 0], ln_b[i, 0],
                                               w_router[i], b_router[i], alpha)
        xf = _moe_tail(i, x1, x1p, topi, gcol, cnt, p[i].reshape(n, -1), ln_g[i, 1], ln_b[i, 1],
                       w_gu, b_gu, w_dn, b_dn,
                       w_ple_gate[i].astype(BF16), w_ple_proj[i].astype(BF16), alpha)
    return xf.reshape(bsz, seq, d)
```

```python
import functools

import jax
import jax.numpy as jnp
from jax import lax
from jax.experimental import pallas as pl
from jax.experimental.pallas import tpu as pltpu

F32 = jnp.float32
BF16 = jnp.bfloat16
U32 = jnp.uint32
I32 = jnp.int32

GATE_SOFTCAP = 15.0
CONV_WIDTH = 3
TOP_K = 4
SWIGLU_ALPHA = 1.702
SWIGLU_LIMIT = 7.0
LN_EPS = 1e-5
RMS_EPS = 1e-6

LANES = 128
SUBLANES = 8
MXU_COLS = 256
VMEM_LIMIT_BYTES = 60000 * 1024

MLSTM_ROWS = 256
PROJ_TM = 1024
PROJ_TN_A = 1024
PROJ_TN_B = 512
POST_TM = 512
POS_TM = 512
DISPATCH_TM = 1024
EXPERT_ROWS = 512
EXPERT_UP_COLS = 512
COMBINE_TM = 256
DMA_UNROLL = 8


def _dot(a, b):
    return jnp.dot(a, b, preferred_element_type=F32)


def _dot_nt(a, b):
    return lax.dot_general(a, b, (((1,), (1,)), ((), ())), preferred_element_type=F32)


def _dot_tn(a, b):
    return lax.dot_general(a, b, (((0,), (0,)), ((), ())), preferred_element_type=F32)


def _split2(x):
    hi = x.astype(BF16)
    lo = (x - hi.astype(F32)).astype(BF16)
    return hi, lo


def _split3(x):
    hi = x.astype(BF16)
    r = x - hi.astype(F32)
    mid = r.astype(BF16)
    lo = (r - mid.astype(F32)).astype(BF16)
    return hi, mid, lo


def _pack_halves(y):
    half = y.shape[1] // 2
    lo = lax.bitcast_convert_type(y[:, :half].astype(BF16).astype(F32), U32) >> 16
    hi = lax.bitcast_convert_type(y[:, half:].astype(BF16).astype(F32), U32) & jnp.uint32(0xFFFF0000)
    return lo | hi


def _unpack_halves(u):
    lo = lax.bitcast_convert_type(u << 16, F32)
    hi = lax.bitcast_convert_type(u & jnp.uint32(0xFFFF0000), F32)
    return lo, hi


def _store_row_tiles(ref, words, first=0, block_rows=None):
    r = words.shape[0]
    c = ref.shape[0] // r if block_rows is None else ref.shape[0] // block_rows
    for j in range(words.shape[1] // LANES):
        ref[pl.ds(first + j, r, stride=c), :] = words[:, j * LANES:(j + 1) * LANES]


def _load_row_tiles(ref, r, index=(), take=None):
    c = ref.shape[-2] // r
    return [ref[index + (pl.ds(j, take or r, stride=c), slice(None))] for j in range(c)]


def _layer_norm(y, g, b):
    mu = jnp.mean(y, axis=-1, keepdims=True)
    yc = y - mu
    var = jnp.mean(yc * yc, axis=-1, keepdims=True)
    return yc * lax.rsqrt(var + LN_EPS) * g + b


def _params(n_axes):
    return pltpu.CompilerParams(dimension_semantics=("arbitrary",) * n_axes,
                                vmem_limit_bytes=VMEM_LIMIT_BYTES)


def _proj_gate_kernel(x_ref, w_ref, wg_ref, bg_ref, o_ref, col_ref, row_ref, xb_ref, *, heads, rows):
    tm = x_ref.shape[0]

    @pl.when(pl.program_id(1) == 0)
    def _():
        x = x_ref[...]
        xh = x.astype(BF16)
        xb_ref[...] = xh
        xl = (x - xh.astype(F32)).astype(BF16)
        wh, wl = _split2(wg_ref[...])
        g = _dot(xh, wh) + _dot(xh, wl) + _dot(xl, wh) + bg_ref[...]
        g = GATE_SOFTCAP * jnp.tanh(g / GATE_SOFTCAP)
        lf = -jnp.log1p(jnp.exp(-g))
        r_i = lax.broadcasted_iota(I32, (rows, rows), 0)
        c_i = lax.broadcasted_iota(I32, (rows, rows), 1)
        tril = (c_i <= r_i).astype(BF16)
        lane = lax.broadcasted_iota(I32, (rows, LANES), 1)
        for c in range(tm // rows):
            rs = slice(c * rows, (c + 1) * rows)
            p0, p1, p2 = _split3(lf[rs])
            bsum = _dot(tril, p0) + _dot(tril, p1) + _dot(tril, p2)
            col = jnp.where(lane < heads, g[rs], jnp.where(lane < 2 * heads, bsum, 0.0))
            col_ref[rs, :] = col
            row_ref[:, rs] = col.T[: row_ref.shape[0]]

    o_ref[...] = _dot(xb_ref[...], w_ref[...]).astype(o_ref.dtype)


def _proj_gates(xf, w, w_g, b_g, heads):
    n, d = xf.shape
    m = w.shape[1]
    tm, tn = PROJ_TM, PROJ_TN_A
    while m % tn:
        tn //= 2
    r_pad = -(-2 * heads // SUBLANES) * SUBLANES
    wg = jnp.zeros((d, LANES), F32).at[:, : 2 * heads].set(w_g)
    bg = jnp.zeros((1, LANES), F32).at[0, : 2 * heads].set(b_g)
    return pl.pallas_call(
        functools.partial(_proj_gate_kernel, heads=heads, rows=MLSTM_ROWS),
        grid=(n // tm, m // tn),
        in_specs=[pl.BlockSpec((tm, d), lambda i, j: (i, 0)),
                  pl.BlockSpec((d, tn), lambda i, j: (0, j)),
                  pl.BlockSpec((d, LANES), lambda i, j: (0, 0)),
                  pl.BlockSpec((1, LANES), lambda i, j: (0, 0))],
        out_specs=[pl.BlockSpec((tm, tn), lambda i, j: (i, j)),
                   pl.BlockSpec((tm, LANES), lambda i, j: (i, 0)),
                   pl.BlockSpec((r_pad, tm), lambda i, j: (0, i))],
        out_shape=[jax.ShapeDtypeStruct((n, m), BF16),
                   jax.ShapeDtypeStruct((n, LANES), F32),
                   jax.ShapeDtypeStruct((r_pad, n), F32)],
        scratch_shapes=[pltpu.VMEM((tm, d), BF16)],
        compiler_params=_params(2),
    )(xf, w, wg, bg)


def _mlstm_kernel(q_ref, k_ref, v_ref, o_ref, col_ref, row_ref, norm_ref, out_ref, c_ref, n_ref, *, heads, dk, dv):
    rows = q_ref.shape[0]
    scale = dk ** -0.5

    @pl.when(pl.program_id(1) == 0)
    def _():
        c_ref[...] = jnp.zeros_like(c_ref)
        n_ref[...] = jnp.zeros_like(n_ref)

    col = col_ref[...]
    row = row_ref[...]
    t_i = lax.broadcasted_iota(I32, (rows, rows), 0)
    s_i = lax.broadcasted_iota(I32, (rows, rows), 1)
    causal = s_i <= t_i
    for h in range(heads):
        q = q_ref[:, h * dk:(h + 1) * dk]
        k = k_ref[:, h * dk:(h + 1) * dk]
        v = v_ref[:, h * dv:(h + 1) * dv]
        li_c = col[:, h:h + 1]
        b_c = col[:, heads + h:heads + h + 1]
        li_r = row[h:h + 1, :]
        b_r = row[heads + h:heads + h + 1, :]
        b_last = b_c[rows - 1:rows, :]
        decay_w = jnp.where(causal, jnp.exp(b_c - b_r + li_r), 0.0)
        sw = _dot_nt(q, k) * scale * decay_w
        eb = jnp.exp(b_c) * scale
        c_prev = c_ref[h]
        n_prev = n_ref[h]
        num = _dot(sw.astype(BF16), v) + eb * _dot(q, c_prev.astype(BF16))
        qn = jnp.sum(q.astype(F32) * n_prev, axis=-1, keepdims=True)
        den = jnp.sum(sw, axis=-1, keepdims=True) + eb * qn
        hh = num / jnp.maximum(jnp.abs(den), 1.0)
        wk = jnp.exp(b_last - b_c + li_c)
        carry = jnp.exp(b_last)
        kf = k.astype(F32) * wk
        c_ref[h] = carry * c_prev + _dot_tn(kf.astype(BF16), v)
        n_ref[h] = carry * n_prev + jnp.sum(kf, axis=0, keepdims=True)
        hh = hh * lax.rsqrt(jnp.mean(hh * hh, axis=-1, keepdims=True) + RMS_EPS)
        hh = hh * norm_ref[:, h * dv:(h + 1) * dv]
        og = o_ref[:, h * dv:(h + 1) * dv].astype(F32)
        out_ref[:, h * dv:(h + 1) * dv] = (hh * jax.nn.sigmoid(og)).astype(out_ref.dtype)


def _mlstm(proj, col, row, norm, bsz, seq, heads, dk, dv):
    rows = MLSTM_ROWS
    nc = seq // rows
    qk_w, v_w = heads * dk, heads * dv
    assert v_w == 2 * qk_w
    r_pad = row.shape[0]
    tok = lambda b, c: b * nc + c
    return pl.pallas_call(
        functools.partial(_mlstm_kernel, heads=heads, dk=dk, dv=dv),
        grid=(bsz, nc),
        in_specs=[pl.BlockSpec((rows, qk_w), lambda b, c: (tok(b, c), 0)),
                  pl.BlockSpec((rows, qk_w), lambda b, c: (tok(b, c), 1)),
                  pl.BlockSpec((rows, v_w), lambda b, c: (tok(b, c), 1)),
                  pl.BlockSpec((rows, v_w), lambda b, c: (tok(b, c), 2)),
                  pl.BlockSpec((rows, LANES), lambda b, c: (tok(b, c), 0)),
                  pl.BlockSpec((r_pad, rows), lambda b, c: (0, tok(b, c))),
                  pl.BlockSpec((1, v_w), lambda b, c: (0, 0))],
        out_specs=pl.BlockSpec((rows, v_w), lambda b, c: (tok(b, c), 0)),
        out_shape=jax.ShapeDtypeStruct((bsz * seq, v_w), BF16),
        scratch_shapes=[pltpu.VMEM((heads, dk, dv), F32), pltpu.VMEM((heads, 1, dk), F32)],
        compiler_params=_params(2),
    )(proj, proj, proj, proj, col, row, norm.reshape(1, v_w))


def _conv_kernel(x_ref, wb_ref, wc_ref, wu_ref, cw_ref, o_ref, xb_ref, carry_ref, *, steps_per_seq):
    i, j = pl.program_id(0), pl.program_id(1)
    tm = x_ref.shape[0]

    @pl.when(j == 0)
    def _():
        xb_ref[...] = x_ref[...].astype(BF16)

    xb = xb_ref[...]
    z = _dot(xb, wc_ref[...]) * _dot(xb, wu_ref[...])
    @pl.when(i % steps_per_seq == 0)
    def _():
        carry_ref[j] = jnp.zeros(carry_ref.shape[1:], F32)

    prev = carry_ref[j]
    carry_ref[j] = z[tm - SUBLANES:, :]
    r_i = lax.broadcasted_iota(I32, z.shape, 0)
    z1 = jnp.where(r_i == 0, prev[SUBLANES - 1:SUBLANES, :], pltpu.roll(z, 1, 0))
    z2 = jnp.where(r_i == 0, prev[SUBLANES - 2:SUBLANES - 1, :],
                   jnp.where(r_i == 1, prev[SUBLANES - 1:SUBLANES, :], pltpu.roll(z, 2, 0)))
    cw = cw_ref[...]
    zc = cw[0:1, :] * z2 + cw[1:2, :] * z1 + cw[2:3, :] * z
    o_ref[...] = (_dot(xb, wb_ref[...]) * zc).astype(o_ref.dtype)


def _conv_front(xf, w_in, conv_w, seq):
    n, d = xf.shape
    tm, tn = PROJ_TM, PROJ_TN_B
    nj = d // tn
    cw = jnp.zeros((SUBLANES, d), F32).at[:CONV_WIDTH].set(conv_w)
    return pl.pallas_call(
        functools.partial(_conv_kernel, steps_per_seq=seq // tm),
        grid=(n // tm, nj),
        in_specs=[pl.BlockSpec((tm, d), lambda i, j: (i, 0)),
                  pl.BlockSpec((d, tn), lambda i, j: (0, j)),
                  pl.BlockSpec((d, tn), lambda i, j: (0, nj + j)),
                  pl.BlockSpec((d, tn), lambda i, j: (0, 2 * nj + j)),
                  pl.BlockSpec((SUBLANES, tn), lambda i, j: (0, j))],
        out_specs=pl.BlockSpec((tm, tn), lambda i, j: (i, j)),
        out_shape=jax.ShapeDtypeStruct((n, d), BF16),
        scratch_shapes=[pltpu.VMEM((tm, d), BF16), pltpu.VMEM((nj, SUBLANES, tn), F32)],
        compiler_params=_params(2),
    )(xf, w_in, w_in, w_in, cw)


def _post_kernel(a_ref, w_ref, x_ref, g_ref, b_ref, wr_ref, br_ref,
                 x1_ref, x1p_ref, topi_ref, gcol_ref, cnt_ref, *, alpha, e_pad):
    tm = a_ref.shape[0]
    y = alpha * x_ref[...] + _dot(a_ref[...], w_ref[...])
    x1 = _layer_norm(y, g_ref[...], b_ref[...])
    x1_ref[...] = x1
    _store_row_tiles(x1p_ref, _pack_halves(x1))
    xh, xl = _split2(x1)
    w2 = wr_ref[...]
    l2 = _dot(xh, w2)
    logits = l2[:, :LANES] + l2[:, LANES:] + _dot(xl, w2[:, :LANES]) + br_ref[...]
    lt = logits.T[:e_pad]
    e_i = lax.broadcasted_iota(I32, lt.shape, 0)
    cur = lt
    ids, vals = [], []
    for _ in range(TOP_K):
        m = jnp.max(cur, axis=0, keepdims=True)
        idx = jnp.min(jnp.where(cur == m, e_i, e_pad), axis=0, keepdims=True)
        cur = jnp.where(e_i == idx, -jnp.inf, cur)
        ids.append(idx)
        vals.append(m)
    ex = [jnp.exp(v - vals[0]) for v in vals]
    den = ex[0] + ex[1] + ex[2] + ex[3]
    k8 = lax.broadcasted_iota(I32, (SUBLANES, tm), 0)
    k128 = lax.broadcasted_iota(I32, (LANES, tm), 0)
    topi = jnp.zeros((SUBLANES, tm), I32)
    gmat = jnp.zeros((LANES, tm), F32)
    for kk in range(TOP_K):
        topi = jnp.where(k8 == kk, ids[kk], topi)
        gmat = jnp.where(k128 == kk, ex[kk] / den, gmat)
    topi_ref[...] = topi
    gcol_ref[...] = gmat.T
    sel = (cur == -jnp.inf).astype(F32)
    cnt_ref[0] = jnp.broadcast_to(jnp.sum(sel, axis=1, keepdims=True), (e_pad, LANES))


def _post_mixer(a, w_out, xf, ln_g, ln_b, w_router, b_router, alpha):
    n, d = xf.shape
    tm = POST_TM
    e = w_router.shape[1]
    assert TOP_K <= e <= LANES
    e_pad = -(-e // SUBLANES) * SUBLANES
    wr = jnp.zeros((d, LANES), F32).at[:, :e].set(w_router)
    wr_hi = wr.astype(BF16)
    wr_lo = (wr - wr_hi.astype(F32)).astype(BF16)
    wr2 = jnp.concatenate([wr_hi, wr_lo], axis=1)
    br = jnp.full((1, LANES), -1e30, F32).at[0, :e].set(b_router)
    full = lambda i: (0, 0)
    once = pl.Buffered(1)
    chunks = d // 2 // LANES
    return pl.pallas_call(
        functools.partial(_post_kernel, alpha=alpha, e_pad=e_pad),
        grid=(n // tm,),
        in_specs=[pl.BlockSpec((tm, d), lambda i: (i, 0)),
                  pl.BlockSpec((d, d), full, pipeline_mode=once),
                  pl.BlockSpec((tm, d), lambda i: (i, 0)),
                  pl.BlockSpec((1, d), full),
                  pl.BlockSpec((1, d), full),
                  pl.BlockSpec((d, 2 * LANES), full, pipeline_mode=once),
                  pl.BlockSpec((1, LANES), full)],
        out_specs=[pl.BlockSpec((tm, d), lambda i: (i, 0)),
                   pl.BlockSpec((tm * chunks, LANES), lambda i: (i, 0)),
                   pl.BlockSpec((SUBLANES, tm), lambda i: (0, i)),
                   pl.BlockSpec((tm, LANES), lambda i: (i, 0)),
                   pl.BlockSpec((1, e_pad, LANES), lambda i: (i, 0, 0))],
        out_shape=[jax.ShapeDtypeStruct((n, d), F32),
                   jax.ShapeDtypeStruct((n * chunks, LANES), U32),
                   jax.ShapeDtypeStruct((SUBLANES, n), I32),
                   jax.ShapeDtypeStruct((n, LANES), F32),
                   jax.ShapeDtypeStruct((n // tm, e_pad, LANES), F32)],
        compiler_params=_params(1),
    )(a, w_out, xf, ln_g.reshape(1, d), ln_b.reshape(1, d), wr2, br)


def _pos_kernel(topi_ref, off_ref, pos_ref, *, e_pad):
    tm = topi_ref.shape[1]
    ti = topi_ref[...]
    e_i = lax.broadcasted_iota(I32, (e_pad, tm), 0)
    hits = [e_i == ti[kk:kk + 1, :] for kk in range(TOP_K)]
    sel = hits[0] | hits[1] | hits[2] | hits[3]
    s_i = lax.broadcasted_iota(I32, (tm, tm), 0)
    t_i = lax.broadcasted_iota(I32, (tm, tm), 1)
    before = (s_i < t_i).astype(BF16)
    cum = _dot(sel.astype(BF16), before) + off_ref[0]
    k8 = lax.broadcasted_iota(I32, (SUBLANES, tm), 0)
    pos = jnp.zeros((SUBLANES, tm), F32)
    for kk in range(TOP_K):
        pk = jnp.sum(jnp.where(hits[kk], cum, 0.0), axis=0, keepdims=True)
        pos = jnp.where(k8 == kk, pk, pos)
    pos_ref[...] = pos.astype(I32)


def _positions(topi, blk_off, e_pad):
    n = topi.shape[1]
    tm = POS_TM
    return pl.pallas_call(
        functools.partial(_pos_kernel, e_pad=e_pad),
        grid=(n // tm,),
        in_specs=[pl.BlockSpec((SUBLANES, tm), lambda i: (0, i)),
                  pl.BlockSpec((1, e_pad, 1), lambda i: (i, 0, 0))],
        out_specs=pl.BlockSpec((SUBLANES, tm), lambda i: (0, i)),
        out_shape=jax.ShapeDtypeStruct((SUBLANES, n), I32),
        compiler_params=_params(1),
    )(topi, blk_off)


def _blocked_pos(pos, tm):
    n = pos.shape[1]
    p = pos[:TOP_K].reshape(TOP_K, n // tm, tm)
    return jnp.transpose(p, (1, 0, 2)).reshape(n // tm, 1, TOP_K * tm)


def _dispatch_kernel(fill_ref, pos_ref, x_ref, xs_ref, zero_ref, sem, fill_sem, *, chunks):
    tm = x_ref.shape[0] // chunks
    tb = zero_ref.shape[0] // chunks

    def rows_of(ref, first, count):
        return ref.at[pl.ds(pl.multiple_of(first * chunks, chunks), count * chunks)]

    @pl.when(pl.program_id(0) == 0)
    def _():
        zero_ref[...] = jnp.zeros_like(zero_ref)

        def each_copy(b, action):
            cnt = fill_ref[b]
            row = (b + 1) * tb - cnt
            bit = tb
            while bit:
                @pl.when((cnt & bit) != 0)
                def _(row=row, bit=bit):
                    action(pltpu.make_async_copy(rows_of(zero_ref, 0, bit), rows_of(xs_ref, row, bit), fill_sem))

                row = row + (cnt & bit)
                bit //= 2

        def start_all(b, carry):
            each_copy(b, lambda cp: cp.start())
            return carry

        def wait_all(b, carry):
            each_copy(b, lambda cp: cp.wait())
            return carry

        lax.fori_loop(0, fill_ref.shape[0], start_all, 0)
        lax.fori_loop(0, fill_ref.shape[0], wait_all, 0)

    def issue(g, carry):
        for u in range(DMA_UNROLL):
            t = pl.multiple_of(g * DMA_UNROLL, DMA_UNROLL) + u
            for kk in range(TOP_K):
                p = pos_ref[0, 0, kk * tm + t]
                pltpu.make_async_copy(rows_of(x_ref, t, 1), rows_of(xs_ref, p, 1), sem).start(priority=(u + kk) % 2)
        return carry

    lax.fori_loop(0, tm // DMA_UNROLL, issue, 0)
    for _ in range(TOP_K):
        pltpu.make_async_copy(x_ref, rows_of(xs_ref, 0, tm), sem).wait()


def _dispatch(blk_fill, pos_blocked, x1p, n_rows, chunks):
    n = x1p.shape[0] // chunks
    tm = DISPATCH_TM
    grid_spec = pltpu.PrefetchScalarGridSpec(
        num_scalar_prefetch=1,
        grid=(n // tm,),
        in_specs=[pl.BlockSpec((1, 1, TOP_K * tm), lambda i, fl: (i, 0, 0), memory_space=pltpu.SMEM),
                  pl.BlockSpec((tm * chunks, LANES), lambda i, fl: (i, 0))],
        out_specs=pl.BlockSpec(memory_space=pl.ANY),
        scratch_shapes=[pltpu.VMEM((EXPERT_ROWS * chunks, LANES), U32), pltpu.SemaphoreType.DMA(()),
                        pltpu.SemaphoreType.DMA(())],
    )
    return pl.pallas_call(
        functools.partial(_dispatch_kernel, chunks=chunks),
        grid_spec=grid_spec,
        out_shape=jax.ShapeDtypeStruct((n_rows * chunks, LANES), U32),
        compiler_params=_params(1),
    )(blk_fill, pos_blocked, x1p)


def _expert_up_kernel(blk_ref, col_ref, exp_ref, flag_ref, xs_ref, wg_ref, wu_ref, bg_ref, bu_ref, act_ref,
                      wgb_ref, wub_ref):
    del blk_ref, col_ref, exp_ref
    flag = flag_ref[pl.program_id(0)]
    tb = act_ref.shape[0]

    @pl.when((flag & 2) != 0)
    def _():
        wgb_ref[...] = wg_ref[...].astype(BF16)
        wub_ref[...] = wu_ref[...].astype(BF16)

    def compute(rows):
        halves = [_unpack_halves(u) for u in _load_row_tiles(xs_ref, tb, take=rows)]
        xb = jnp.concatenate([lo.astype(BF16) for lo, _ in halves] + [hi.astype(BF16) for _, hi in halves], axis=1)
        for c in range(0, act_ref.shape[1], MXU_COLS):
            cs = slice(c, c + MXU_COLS)
            g = jnp.minimum(_dot(xb, wgb_ref[:, cs]) + bg_ref[:, cs], SWIGLU_LIMIT)
            u = jnp.clip(_dot(xb, wub_ref[:, cs]) + bu_ref[:, cs], -SWIGLU_LIMIT, SWIGLU_LIMIT)
            act_ref[:rows, cs] = ((u + 1.0) * (g * jax.nn.sigmoid(SWIGLU_ALPHA * g))).astype(act_ref.dtype)

    @pl.when((flag & 5) == 1)
    def _():
        compute(tb)

    @pl.when((flag & 5) == 5)
    def _():
        compute(tb // 2)
        act_ref[tb // 2:, :] = jnp.zeros((tb - tb // 2, act_ref.shape[1]), act_ref.dtype)

    @pl.when((flag & 1) == 0)
    def _():
        act_ref[...] = jnp.zeros_like(act_ref)


def _expert_down_kernel(be_ref, nu_ref, first_ref, act_ref, wd_ref, bd_ref, ys_ref, wdb_ref):
    del be_ref
    i = pl.program_id(0)

    flag = first_ref[i]
    tb = act_ref.shape[0]

    @pl.when((flag & 1) != 0)
    def _():
        wdb_ref[...] = wd_ref[...].astype(BF16)

    def compute(rows):
        act = act_ref[:rows, :]
        half = wdb_ref.shape[1] // 2
        for c in range(0, half, MXU_COLS):
            lo = _dot(act, wdb_ref[:, c:c + MXU_COLS]) + bd_ref[:, c:c + MXU_COLS]
            hi = _dot(act, wdb_ref[:, half + c:half + c + MXU_COLS]) + bd_ref[:, half + c:half + c + MXU_COLS]
            _store_row_tiles(ys_ref, _pack_halves(jnp.concatenate([lo, hi], axis=1)), first=c // LANES,
                             block_rows=tb)

    @pl.when((i < nu_ref[0]) & ((flag & 2) == 0))
    def _():
        compute(tb)

    @pl.when((i < nu_ref[0]) & ((flag & 2) != 0))
    def _():
        compute(tb // 2)
        tail = ys_ref.shape[0] // tb * (tb // 2)
        ys_ref[tail:, :] = jnp.zeros((ys_ref.shape[0] - tail, ys_ref.shape[1]), ys_ref.dtype)

    @pl.when(i >= nu_ref[0])
    def _():
        ys_ref[...] = jnp.zeros_like(ys_ref)


def _experts(layer, sched, xs, w_gu, b_gu, w_dn, b_dn):
    _, e, d, f2 = w_gu.shape
    chunks = d // 2 // LANES
    n_rows = xs.shape[0] // chunks
    f = f2 // 2
    tb, fh = EXPERT_ROWS, EXPERT_UP_COLS
    nh = f // fh
    nb = n_rows // tb
    up_spec = pltpu.PrefetchScalarGridSpec(
        num_scalar_prefetch=4,
        grid=(nh * nb,),
        in_specs=[pl.BlockSpec((tb * chunks, LANES), lambda t, bl, co, ex, fl: (bl[t], 0)),
                  pl.BlockSpec((None, None, d, fh), lambda t, bl, co, ex, fl: (layer, ex[t], 0, co[t])),
                  pl.BlockSpec((None, None, d, fh), lambda t, bl, co, ex, fl: (layer, ex[t], 0, nh + co[t])),
                  pl.BlockSpec((None, None, 1, fh), lambda t, bl, co, ex, fl: (layer, ex[t], 0, co[t])),
                  pl.BlockSpec((None, None, 1, fh), lambda t, bl, co, ex, fl: (layer, ex[t], 0, nh + co[t]))],
        out_specs=pl.BlockSpec((tb, fh), lambda t, bl, co, ex, fl: (bl[t], co[t])),
        scratch_shapes=[pltpu.VMEM((d, fh), BF16), pltpu.VMEM((d, fh), BF16)],
    )
    b_gu4 = b_gu.reshape(b_gu.shape[0], e, 1, f2)
    act = pl.pallas_call(
        _expert_up_kernel,
        grid_spec=up_spec,
        out_shape=jax.ShapeDtypeStruct((n_rows, f), BF16),
        compiler_params=_params(1),
    )(sched["it_blk"], sched["it_col"], sched["it_exp"], sched["it_flag"], xs, w_gu, w_gu, b_gu4, b_gu4)

    def blk(i, nu):
        return jnp.minimum(i, nu[0] - 1)

    down_spec = pltpu.PrefetchScalarGridSpec(
        num_scalar_prefetch=3,
        grid=(nb,),
        in_specs=[pl.BlockSpec((tb, f), lambda i, be, nu, fi: (blk(i, nu), 0)),
                  pl.BlockSpec((None, None, f, d), lambda i, be, nu, fi: (layer, be[i], 0, 0)),
                  pl.BlockSpec((None, None, 1, d), lambda i, be, nu, fi: (layer, be[i], 0, 0))],
        out_specs=pl.BlockSpec((tb * chunks, LANES), lambda i, be, nu, fi: (i, 0)),
        scratch_shapes=[pltpu.VMEM((f, d), BF16)],
    )
    return pl.pallas_call(
        _expert_down_kernel,
        grid_spec=down_spec,
        out_shape=jax.ShapeDtypeStruct((n_rows * chunks, LANES), U32),
        compiler_params=_params(1),
    )(sched["blk_e"], sched["n_used"], sched["blk_first"], act, w_dn, b_dn.reshape(b_dn.shape[0], e, 1, d))


def _expert_schedule(total, n_rows, e, f):
    tb = EXPERT_ROWS
    nh = f // EXPERT_UP_COLS
    nb = n_rows // tb
    padded = (total + tb - 1) // tb * tb
    pad_end = jnp.cumsum(padded)
    pad_start = pad_end - padded
    n_used = pad_end[-1] // tb
    blocks = jnp.arange(nb, dtype=I32)
    blk_e = jnp.minimum(jnp.sum(pad_end[None, :] <= (blocks * tb)[:, None], axis=1), e - 1).astype(I32)
    blk_e = jnp.where(blocks < n_used, blk_e, blk_e[n_used - 1])
    blk_first = jnp.concatenate([jnp.ones((1,), I32), (blk_e[1:] != blk_e[:-1]).astype(I32)])
    grp_last = blocks == pad_end[blk_e] // tb - 1
    blk_fill = jnp.where(blocks < n_used, jnp.where(grp_last, (padded - total)[blk_e], 0), tb).astype(I32)
    blk_half = ((blocks < n_used) & (blk_fill >= tb - tb // 2)).astype(I32)
    blk_first = blk_first + 2 * blk_half
    items = jnp.arange(nh * nb, dtype=I32)
    grp_end = nh * (pad_end // tb)
    it_e = jnp.minimum(jnp.sum(grp_end[None, :] <= items[:, None], axis=1), e - 1).astype(I32)
    grp_blocks = jnp.maximum((padded // tb)[it_e], 1)
    r = items - nh * (pad_start // tb)[it_e]
    valid = items < nh * n_used
    spare = items - nh * n_used
    it_blk = jnp.where(valid, (pad_start // tb)[it_e] + r % grp_blocks, n_used + spare // nh)
    it_col = jnp.where(valid, r // grp_blocks, spare % nh)
    it_exp = jnp.where(valid, it_e, blk_e[n_used - 1])
    it_flag = jnp.where(valid, 1 + 2 * (r % grp_blocks == 0).astype(I32) + 4 * blk_half[it_blk], 0)
    it_flag = it_flag.at[0].set(it_flag[0] | 2)
    return dict(pad_start=pad_start, blk_fill=blk_fill, blk_e=blk_e, n_used=n_used.astype(I32).reshape(1),
                blk_first=blk_first, it_blk=it_blk.astype(I32), it_col=it_col.astype(I32),
                it_exp=it_exp.astype(I32), it_flag=it_flag.astype(I32))


def _combine_kernel(pos_ref, nxt_ref, ys_ref, x1_ref, gcol_ref, p_ref, g_ref, b_ref, wpg_ref, wpp_ref, out_ref,
                    buf_ref, sem, *, alpha):
    tm = x1_ref.shape[0]
    chunks = buf_ref.shape[2] // tm
    i = pl.program_id(0)
    slot = i % 2
    d = out_ref.shape[1]

    def row_copy(rows_ref, t, kk, dst):
        r = rows_ref[0, 0, kk * tm + t]
        first = t * chunks if isinstance(t, int) else pl.multiple_of(t * chunks, chunks)
        return pltpu.make_async_copy(ys_ref.at[pl.ds(pl.multiple_of(r * chunks, chunks), chunks)],
                                     buf_ref.at[dst, kk, pl.ds(first, chunks)], sem.at[dst])

    def wait_rows(dst):
        for kk in range(TOP_K):
            pltpu.make_async_copy(ys_ref.at[pl.ds(0, tm * chunks)], buf_ref.at[dst, kk], sem.at[dst]).wait()

    @pl.when(i == 0)
    def _():
        def issue(g, carry):
            for u in range(DMA_UNROLL):
                for kk in range(TOP_K):
                    row_copy(pos_ref, pl.multiple_of(g * DMA_UNROLL, DMA_UNROLL) + u, kk, 0).start(priority=(u + kk) % 2)
            return carry

        lax.fori_loop(0, tm // DMA_UNROLL, issue, 0)

    wait_rows(slot)

    copies = [(t, kk) for t in range(tm) for kk in range(TOP_K)]
    n_stages = d // MXU_COLS
    per_stage = -(-len(copies) // n_stages)

    def start_next(stage):
        for j, (t, kk) in enumerate(copies[stage * per_stage:(stage + 1) * per_stage]):
            row_copy(nxt_ref, t, kk, 1 - slot).start(priority=j % 2)

    gates = gcol_ref[...]
    lo = hi = None
    for kk in range(TOP_K):
        gk = gates[:, kk:kk + 1]
        halves = [_unpack_halves(u) for u in _load_row_tiles(buf_ref, tm, (slot, kk))]
        lo = [gk * l for l, _ in halves] if lo is None else [a + gk * l for a, (l, _) in zip(lo, halves)]
        hi = [gk * h for _, h in halves] if hi is None else [a + gk * h for a, (_, h) in zip(hi, halves)]
    ffn = jnp.concatenate(lo + hi, axis=1)
    x2 = _layer_norm(alpha * x1_ref[...] + ffn, g_ref[...], b_ref[...])
    x2b = x2.astype(BF16)
    pb = p_ref[...].astype(BF16)
    for c in range(d // MXU_COLS):
        start_next(c)
        cs = slice(c * MXU_COLS, (c + 1) * MXU_COLS)
        gate = jax.nn.sigmoid(_dot(x2b, wpg_ref[:, cs]))
        out_ref[:, cs] = x2[:, cs] + gate * _dot(pb, wpp_ref[:, cs])

    @pl.when(i == pl.num_programs(0) - 1)
    def _():
        wait_rows(1 - slot)


def _combine(pos_blocked, ys, x1, gcol, p, ln_g, ln_b, w_pg, w_pp, alpha):
    n, d = x1.shape
    tm = COMBINE_TM
    pd = p.shape[1]
    full = lambda i: (0, 0)
    once = pl.Buffered(1)
    last = n // tm - 1
    return pl.pallas_call(
        functools.partial(_combine_kernel, alpha=alpha),
        grid=(n // tm,),
        in_specs=[pl.BlockSpec((1, 1, TOP_K * tm), lambda i: (i, 0, 0), memory_space=pltpu.SMEM),
                  pl.BlockSpec((1, 1, TOP_K * tm), lambda i: (jnp.minimum(i + 1, last), 0, 0),
                               memory_space=pltpu.SMEM),
                  pl.BlockSpec(memory_space=pl.ANY),
                  pl.BlockSpec((tm, d), lambda i: (i, 0)),
                  pl.BlockSpec((tm, LANES), lambda i: (i, 0)),
                  pl.BlockSpec((tm, pd), lambda i: (i, 0)),
                  pl.BlockSpec((1, d), full),
                  pl.BlockSpec((1, d), full),
                  pl.BlockSpec((d, d), full, pipeline_mode=once),
                  pl.BlockSpec((pd, d), full, pipeline_mode=once)],
        out_specs=pl.BlockSpec((tm, d), lambda i: (i, 0)),
        out_shape=jax.ShapeDtypeStruct((n, d), F32),
        scratch_shapes=[pltpu.VMEM((2, TOP_K, tm * (d // 2 // LANES), LANES), U32), pltpu.SemaphoreType.DMA((2,))],
        compiler_params=_params(1),
    )(pos_blocked, pos_blocked, ys, x1, gcol, p, ln_g.reshape(1, d), ln_b.reshape(1, d), w_pg, w_pp)


def _moe_tail(layer, x1, x1p, topi, gcol, cnt, p, ln_g, ln_b, w_gu, b_gu, w_dn, b_dn, w_pg, w_pp, alpha):
    n, d = x1.shape
    e = w_gu.shape[1]
    e_pad = cnt.shape[1]
    tb = EXPERT_ROWS
    c = cnt[:, :, 0].astype(I32)
    c = c.reshape(n // POS_TM, POS_TM // POST_TM, e_pad).sum(axis=1)
    n_rows = -(-n * TOP_K // tb) * tb + e * tb
    sched = _expert_schedule(c.sum(axis=0), n_rows, e, w_gu.shape[3] // 2)
    blk_off = (sched["pad_start"][None, :] + jnp.cumsum(c, axis=0) - c).astype(F32)[:, :, None]

    pos = _positions(topi, blk_off, e_pad)
    xs = _dispatch(sched["blk_fill"], _blocked_pos(pos, DISPATCH_TM), x1p, n_rows, d // 2 // LANES)
    ys = _experts(layer, sched, xs, w_gu, b_gu, w_dn, b_dn)
    return _combine(_blocked_pos(pos, COMBINE_TM), ys, x1, gcol, p, ln_g, ln_b, w_pg, w_pp, alpha)


def kernel(x, p, ln_g, ln_b, w_in_a, b_gate_a, norm_a, w_out_a, w_in_b, conv_b, w_out_b, w_router, b_router,
           w_gu, b_gu, w_dn, b_dn, w_ple_gate, w_ple_proj):
    bsz, seq, d = x.shape
    depth = ln_g.shape[0]
    n = bsz * seq
    alpha = (2 * depth) ** 0.25
    heads = b_gate_a.shape[-1] // 2
    dv = d // heads
    dk = dv // 2
    qkvo = 2 * heads * dk + 2 * heads * dv
    xf = x.reshape(n, d)
    for i in range(depth):
        j = i // 2
        if i % 2 == 0:
            proj, col, row = _proj_gates(xf, w_in_a[j][:, :qkvo].astype(BF16), w_in_a[j][:, qkvo:], b_gate_a[j],
                                         heads)
            a = _mlstm(proj, col, row, norm_a[j], bsz, seq, heads, dk, dv)
            w_out = w_out_a[j]
        else:
            a = _conv_front(xf, w_in_b[j].astype(BF16), conv_b[j], seq)
            w_out = w_out_b[j]
        x1, x1p, topi, gcol, cnt = _post_mixer(a, w_out.astype(BF16), xf, ln_g[i, 0], ln_b[i, 0],
                                               w_router[i], b_router[i], alpha)
        xf = _moe_tail(i, x1, x1p, topi, gcol, cnt, p[i].reshape(n, -1), ln_g[i, 1], ln_b[i, 1],
                       w_gu, b_gu, w_dn, b_dn,
                       w_ple_gate[i].astype(BF16), w_ple_proj[i].astype(BF16), alpha)
    return xf.reshape(bsz, seq, d)
```
